```python
import math
import jax, jax.numpy as jnp
from jax import lax
import numpy as np

D_MODEL = 1024
BATCH = 8
SEQ = 4096
DEPTH = 2

N_A_LAYERS = DEPTH // 2
N_B_LAYERS = DEPTH - N_A_LAYERS

MLA_HEADS = 8
MLA_NOPE = 128
MLA_ROPE = 64
MLA_V = 128
MLA_Q_LORA = 384
MLA_KV_LORA = 256

DIFF_HEADS = 8
DIFF_HEAD_DIM = 64
DIFF_QK_WIDTH = DIFF_HEADS * 2 * DIFF_HEAD_DIM
DIFF_V_WIDTH = DIFF_HEADS * 2 * DIFF_HEAD_DIM

D_FF = -(-8 * D_MODEL // (3 * 256)) * 256

ROPE_THETA = 10000.0
Q_BLOCK = 128
DEEPNORM_ALPHA = (2.0 * DEPTH) ** 0.25
DEEPNORM_BETA = (8.0 * DEPTH) ** -0.25
LN_EPS = 1e-5
RMS_EPS = 1e-6

kernel_name = "yoco_mla_diffattn_deepnorm"

F32 = jnp.float32


def _layernorm(x, g, b):
    xf = x.astype(F32)
    mu = jnp.mean(xf, axis=-1, keepdims=True)
    var = jnp.mean(jnp.square(xf - mu), axis=-1, keepdims=True)
    y = (xf - mu) * lax.rsqrt(var + LN_EPS)
    return (y * g.astype(F32) + b.astype(F32)).astype(x.dtype)


def _rmsnorm(x, g):
    xf = x.astype(F32)
    y = xf * lax.rsqrt(jnp.mean(jnp.square(xf), axis=-1, keepdims=True) + RMS_EPS)
    return (y * g.astype(F32)).astype(x.dtype)


def _rope_tables(seq_len, dim):
    inv = 1.0 / (ROPE_THETA ** (jnp.arange(0, dim, 2, dtype=F32) / dim))
    ang = jnp.arange(seq_len, dtype=F32)[:, None] * inv[None, :]
    ang = jnp.concatenate([ang, ang], axis=-1)
    return jnp.cos(ang), jnp.sin(ang)


def _rope(x, cos, sin):
    shape = (1, x.shape[1]) + (1,) * (x.ndim - 3) + (x.shape[-1],)
    c = cos.reshape(shape)
    s = sin.reshape(shape)
    xf = x.astype(F32)
    half = x.shape[-1] // 2
    rot = jnp.concatenate([-xf[..., half:], xf[..., :half]], axis=-1)
    return (xf * c + rot * s).astype(x.dtype)


def _causal_mask(i, seq_len):
    qpos = i * Q_BLOCK + jnp.arange(Q_BLOCK)
    kpos = jnp.arange(seq_len)
    return kpos[None, :] <= qpos[:, None]


def _sweep_query_blocks(block_fn, seq_len):
    out = lax.map(block_fn, jnp.arange(seq_len // Q_BLOCK))
    out = jnp.moveaxis(out, 0, 1)
    return out.reshape((out.shape[0], seq_len) + out.shape[3:])


def _mla(h, w_dq, q_norm, w_uq, w_dkv, kv_norm, w_ukv, w_o, cos, sin):
    B, S, _ = h.shape
    cq = _rmsnorm(h @ w_dq, q_norm)
    q = (cq @ w_uq).reshape(B, S, MLA_HEADS, MLA_NOPE + MLA_ROPE)
    q_nope = q[..., :MLA_NOPE]
    q_rope = _rope(q[..., MLA_NOPE:], cos, sin)
    ckv = h @ w_dkv
    c = _rmsnorm(ckv[..., :MLA_KV_LORA], kv_norm)
    k_rope = _rope(ckv[..., MLA_KV_LORA:], cos, sin)
    kv = (c @ w_ukv).reshape(B, S, MLA_HEADS, MLA_NOPE + MLA_V)
    k_nope = kv[..., :MLA_NOPE]
    v = kv[..., MLA_NOPE:]
    scale = (MLA_NOPE + MLA_ROPE) ** -0.5

    def block(i):
        start = i * Q_BLOCK
        qn = lax.dynamic_slice_in_dim(q_nope, start, Q_BLOCK, axis=1)
        qr = lax.dynamic_slice_in_dim(q_rope, start, Q_BLOCK, axis=1)
        s = (jnp.einsum('bqhd,bkhd->bhqk', qn, k_nope).astype(F32)
             + jnp.einsum('bqhr,bkr->bhqk', qr, k_rope).astype(F32)) * scale
        s = jnp.where(_causal_mask(i, S)[None, None], s, -jnp.inf)
        p = jax.nn.softmax(s, axis=-1).astype(v.dtype)
        return jnp.einsum('bhqk,bkhd->bqhd', p, v)

    o = _sweep_query_blocks(block, S)
    return o.reshape(B, S, MLA_HEADS * MLA_V) @ w_o


def _shared_kv(h, w_kv, cos, sin):
    B, S, _ = h.shape
    kv = h @ w_kv
    k = _rope(kv[..., :DIFF_QK_WIDTH].reshape(B, S, DIFF_HEADS, 2, DIFF_HEAD_DIM), cos, sin)
    v = kv[..., DIFF_QK_WIDTH:].reshape(B, S, DIFF_HEADS, 2 * DIFF_HEAD_DIM)
    return k, v


def _diff_attn(h, k, v, w_q, lq1, lk1, lq2, lk2, subln, w_o, lambda_init, cos, sin):
    B, S, _ = h.shape
    q = _rope((h @ w_q).reshape(B, S, DIFF_HEADS, 2, DIFF_HEAD_DIM), cos, sin)
    lam = (jnp.exp(jnp.sum(lq1.astype(F32) * lk1.astype(F32)))
           - jnp.exp(jnp.sum(lq2.astype(F32) * lk2.astype(F32))) + lambda_init)
    scale = DIFF_HEAD_DIM ** -0.5

    def block(i):
        qb = lax.dynamic_slice_in_dim(q, i * Q_BLOCK, Q_BLOCK, axis=1)
        s = jnp.einsum('bqhcd,bkhcd->bhcqk', qb, k).astype(F32) * scale
        s = jnp.where(_causal_mask(i, S)[None, None, None], s, -jnp.inf)
        p = jax.nn.softmax(s, axis=-1)
        w = (p[:, :, 0] - lam * p[:, :, 1]).astype(v.dtype)
        return jnp.einsum('bhqk,bkhe->bqhe', w, v)

    o = _sweep_query_blocks(block, S)
    o = _rmsnorm(o, subln) * (1.0 - lambda_init)
    return o.reshape(B, S, DIFF_V_WIDTH) @ w_o


def _swiglu(h, w_gate_up, w_down):
    gu = h @ w_gate_up
    g, u = gu[..., :D_FF], gu[..., D_FF:]
    return (jax.nn.silu(g) * u) @ w_down


def _w(k, shape, fan_in, scale=1.0):
    return jax.random.normal(k, shape, F32) * (scale * fan_in ** -0.5)


def setup_inputs(seed: int = 0) -> dict:
    key = jax.random.key(seed)
    ks = iter(jax.random.split(key, 32))
    nA, nB, L = N_A_LAYERS, N_B_LAYERS, DEPTH
    x = jax.random.normal(next(ks), (BATCH, SEQ, D_MODEL), F32)

    mla_w_dq = _w(next(ks), (nA, D_MODEL, MLA_Q_LORA), D_MODEL)
    mla_q_norm = 1.0 + 0.02 * jax.random.normal(next(ks), (nA, MLA_Q_LORA), F32)
    mla_w_uq = _w(next(ks), (nA, MLA_Q_LORA, MLA_HEADS * (MLA_NOPE + MLA_ROPE)), MLA_Q_LORA)
    mla_w_dkv = _w(next(ks), (nA, D_MODEL, MLA_KV_LORA + MLA_ROPE), D_MODEL)
    mla_kv_norm = 1.0 + 0.02 * jax.random.normal(next(ks), (nA, MLA_KV_LORA), F32)
    uk = _w(next(ks), (nA, MLA_KV_LORA, MLA_HEADS, MLA_NOPE), MLA_KV_LORA)
    uv = _w(next(ks), (nA, MLA_KV_LORA, MLA_HEADS, MLA_V), MLA_KV_LORA, DEEPNORM_BETA)
    mla_w_ukv = jnp.concatenate([uk, uv], axis=-1).reshape(nA, MLA_KV_LORA, MLA_HEADS * (MLA_NOPE + MLA_V))
    mla_w_o = _w(next(ks), (nA, MLA_HEADS * MLA_V, D_MODEL), MLA_HEADS * MLA_V, DEEPNORM_BETA)

    wk = _w(next(ks), (D_MODEL, DIFF_QK_WIDTH), D_MODEL)
    wv = _w(next(ks), (D_MODEL, DIFF_V_WIDTH), D_MODEL, DEEPNORM_BETA)
    kv_w = jnp.concatenate([wk, wv], axis=-1)

    diff_w_q = _w(next(ks), (nB, D_MODEL, DIFF_QK_WIDTH), D_MODEL)
    diff_lq1 = 0.1 * jax.random.normal(next(ks), (nB, DIFF_HEAD_DIM), F32)
    diff_lk1 = 0.1 * jax.random.normal(next(ks), (nB, DIFF_HEAD_DIM), F32)
    diff_lq2 = 0.1 * jax.random.normal(next(ks), (nB, DIFF_HEAD_DIM), F32)
    diff_lk2 = 0.1 * jax.random.normal(next(ks), (nB, DIFF_HEAD_DIM), F32)
    diff_subln = 1.0 + 0.02 * jax.random.normal(next(ks), (nB, 2 * DIFF_HEAD_DIM), F32)
    diff_w_o = _w(next(ks), (nB, DIFF_V_WIDTH, D_MODEL), DIFF_V_WIDTH, DEEPNORM_BETA)

    ln1_g = 1.0 + 0.02 * jax.random.normal(next(ks), (L, D_MODEL), F32)
    ln1_b = 0.02 * jax.random.normal(next(ks), (L, D_MODEL), F32)
    ln2_g = 1.0 + 0.02 * jax.random.normal(next(ks), (L, D_MODEL), F32)
    ln2_b = 0.02 * jax.random.normal(next(ks), (L, D_MODEL), F32)
    ffn_w_gate_up = _w(next(ks), (L, D_MODEL, 2 * D_FF), D_MODEL)
    ffn_w_down = _w(next(ks), (L, D_FF, D_MODEL), D_FF, DEEPNORM_BETA)

    return {"x": x,
            "mla_w_dq": mla_w_dq, "mla_q_norm": mla_q_norm, "mla_w_uq": mla_w_uq,
            "mla_w_dkv": mla_w_dkv, "mla_kv_norm": mla_kv_norm, "mla_w_ukv": mla_w_ukv,
            "mla_w_o": mla_w_o,
            "kv_w": kv_w,
            "diff_w_q": diff_w_q, "diff_lq1": diff_lq1, "diff_lk1": diff_lk1,
            "diff_lq2": diff_lq2, "diff_lk2": diff_lk2, "diff_subln": diff_subln,
            "diff_w_o": diff_w_o,
            "ln1_g": ln1_g, "ln1_b": ln1_b, "ln2_g": ln2_g, "ln2_b": ln2_b,
            "ffn_w_gate_up": ffn_w_gate_up, "ffn_w_down": ffn_w_down}


def reference(x, mla_w_dq, mla_q_norm, mla_w_uq, mla_w_dkv, mla_kv_norm, mla_w_ukv, mla_w_o,
              kv_w, diff_w_q, diff_lq1, diff_lk1, diff_lq2, diff_lk2, diff_subln, diff_w_o,
              ln1_g, ln1_b, ln2_g, ln2_b, ffn_w_gate_up, ffn_w_down):
    S = x.shape[1]
    cos_m, sin_m = _rope_tables(S, MLA_ROPE)
    cos_d, sin_d = _rope_tables(S, DIFF_HEAD_DIM)
    h = x
    k_sh = v_sh = None
    for l in range(DEPTH):
        if l < N_A_LAYERS:
            a = _mla(h, mla_w_dq[l], mla_q_norm[l], mla_w_uq[l], mla_w_dkv[l], mla_kv_norm[l],
                     mla_w_ukv[l], mla_w_o[l], cos_m, sin_m)
        else:
            if l == N_A_LAYERS:
                k_sh, v_sh = _shared_kv(h, kv_w, cos_d, sin_d)
            j = l - N_A_LAYERS
            lambda_init = 0.8 - 0.6 * math.exp(-0.3 * l)
            a = _diff_attn(h, k_sh, v_sh, diff_w_q[j], diff_lq1[j], diff_lk1[j], diff_lq2[j],
                           diff_lk2[j], diff_subln[j], diff_w_o[j], lambda_init, cos_d, sin_d)
        h = _layernorm(DEEPNORM_ALPHA * h + a, ln1_g[l], ln1_b[l])
        h = _layernorm(DEEPNORM_ALPHA * h + _swiglu(h, ffn_w_gate_up[l], ffn_w_down[l]), ln2_g[l], ln2_b[l])
    return h
```

```python
import functools
import math

import jax
import jax.numpy as jnp
from jax import lax
from jax.experimental import pallas as pl
from jax.experimental.pallas import tpu as pltpu

F32 = jnp.float32
BF16 = jnp.bfloat16

MLA_HEADS = 8
MLA_NOPE = 128
MLA_ROPE = 64
MLA_V = 128
MLA_Q_LORA = 384
MLA_KV_LORA = 256
DIFF_HEADS = 8
DIFF_HEAD_DIM = 64
ROPE_THETA = 10000.0
DEPTH = 2
DEEPNORM_ALPHA = (2.0 * DEPTH) ** 0.25
LN_EPS = 1e-5
RMS_EPS = 1e-6
LOG2E = math.log2(math.e)
NEG_BIG = -1e30

LANES = 128
TOKEN_TILE = 512
ATTN_TILE = 512
FF_CHUNK = 256
VMEM_LIMIT = 56 * 1024 * 1024


def _nt_dot(a, b):
    return lax.dot_general(a, b, (((1,), (1,)), ((), ())), preferred_element_type=F32)


def _dot(a, b):
    return jnp.dot(a, b, preferred_element_type=F32)


def _rope_chunk(x, cos, sin_signed):
    lane = lax.broadcasted_iota(jnp.int32, x.shape, 1)
    first_half = (lane % 64) < 32
    rot = jnp.where(first_half, pltpu.roll(x, LANES - 32, 1), pltpu.roll(x, 32, 1))
    return x * cos + rot * sin_signed


def _rms(x, g):
    return x * lax.rsqrt(jnp.mean(jnp.square(x), axis=-1, keepdims=True) + RMS_EPS) * g


def _layernorm(x, g, b):
    mu = jnp.mean(x, axis=-1, keepdims=True)
    xc = x - mu
    var = jnp.mean(jnp.square(xc), axis=-1, keepdims=True)
    return xc * lax.rsqrt(var + LN_EPS) * g + b


def _const_spec(shape, single_buffer=False):
    nd = len(shape)
    kw = {"pipeline_mode": pl.Buffered(1)} if single_buffer else {}
    return pl.BlockSpec(shape, lambda *_: (0,) * nd, **kw)


def _mla_proj_kernel(x_ref, cos_ref, sin_ref, w_in_ref, qn_ref, kvn_ref, w_uq_ref, w_uk_ref, w_uvt_ref,
                     q_ref, k_ref, vt_ref, *, q_scale):
    xb = x_ref[0].astype(BF16)
    y = _dot(xb, w_in_ref[...])
    cq = _rms(y[:, :MLA_Q_LORA], qn_ref[...])
    c = _rms(y[:, MLA_Q_LORA:MLA_Q_LORA + MLA_KV_LORA], kvn_ref[...])
    cos = cos_ref[...]
    sin = sin_ref[...]
    k_rope = _rope_chunk(y[:, MLA_Q_LORA + MLA_KV_LORA:], cos, sin).astype(BF16)
    q = _dot(cq.astype(BF16), w_uq_ref[...])
    cb = c.astype(BF16)
    k_nope = _dot(cb, w_uk_ref[...])
    vt = _nt_dot(w_uvt_ref[...], cb)
    rope_base = MLA_HEADS * MLA_NOPE
    for h in range(MLA_HEADS):
        lo, hi = h * LANES, (h + 1) * LANES
        q_ref[0, h, :, 0:LANES] = (q[:, lo:hi] * q_scale).astype(BF16)
        q_ref[0, h, :, LANES:2 * LANES] = (
            _rope_chunk(q[:, rope_base + lo:rope_base + hi], cos, sin) * q_scale).astype(BF16)
        k_ref[0, h, :, 0:LANES] = k_nope[:, lo:hi].astype(BF16)
        k_ref[0, h, :, LANES:2 * LANES] = k_rope
        vt_ref[0, h, 0] = vt[lo:hi, :].astype(BF16)


def _mla_proj(x, cos, sin, w_in, qn, kvn, w_uq, w_uk, w_uvt, q_scale):
    B, S, D = x.shape
    ts = TOKEN_TILE
    H = MLA_HEADS
    out_shape = (
        jax.ShapeDtypeStruct((B, H, S, 2 * LANES), BF16),
        jax.ShapeDtypeStruct((B, H, S, 2 * LANES), BF16),
        jax.ShapeDtypeStruct((B, H, S // ts, MLA_V, ts), BF16),
    )
    return pl.pallas_call(
        functools.partial(_mla_proj_kernel, q_scale=q_scale),
        grid=(B, S // ts),
        in_specs=[
            pl.BlockSpec((1, ts, D), lambda b, s: (b, s, 0)),
            pl.BlockSpec((ts, LANES), lambda b, s: (s, 0)),
            pl.BlockSpec((ts, LANES), lambda b, s: (s, 0)),
            _const_spec(w_in.shape), _const_spec(qn.shape), _const_spec(kvn.shape),
            _const_spec(w_uq.shape), _const_spec(w_uk.shape), _const_spec(w_uvt.shape),
        ],
        out_specs=(
            pl.BlockSpec((1, H, ts, 2 * LANES), lambda b, s: (b, 0, s, 0)),
            pl.BlockSpec((1, H, ts, 2 * LANES), lambda b, s: (b, 0, s, 0)),
            pl.BlockSpec((1, H, 1, MLA_V, ts), lambda b, s: (b, 0, s, 0, 0)),
        ),
        out_shape=out_shape,
        compiler_params=pltpu.CompilerParams(
            dimension_semantics=("parallel", "parallel"), vmem_limit_bytes=VMEM_LIMIT),
        name="mla_proj",
    )(x, cos, sin, w_in, qn, kvn, w_uq, w_uk, w_uvt)


def _flash_loop(q, k_block, vt_block, n_full, tq, m_ref, l_ref, acc_ref):
    m_ref[...] = jnp.full(m_ref.shape, NEG_BIG, F32)
    l_ref[...] = jnp.zeros(l_ref.shape, F32)
    acc_ref[...] = jnp.zeros(acc_ref.shape, F32)

    def step(j, diagonal):
        s = _nt_dot(k_block(j), q)
        if diagonal:
            row = lax.broadcasted_iota(jnp.int32, s.shape, 0)
            col = lax.broadcasted_iota(jnp.int32, s.shape, 1) % tq
            s = jnp.where(row <= col, s, NEG_BIG)
        m_prev = m_ref[...]
        m_new = jnp.maximum(m_prev, jnp.max(s, axis=0, keepdims=True))
        alpha = jnp.exp2(m_prev - m_new)
        p = jnp.exp2(s - m_new)
        l_ref[...] = alpha * l_ref[...] + jnp.sum(p, axis=0, keepdims=True)
        m_ref[...] = m_new
        acc_ref[...] = acc_ref[...] * alpha + _dot(vt_block(j), p.astype(BF16))

    def body(j, carry):
        step(j, False)
        return carry

    lax.fori_loop(0, n_full, body, 0)
    step(n_full, True)


def _mla_attn_kernel(q_ref, k_ref, vt_ref, o_ref, m_ref, l_ref, acc_ref):
    tq = q_ref.shape[2]
    qi = pl.program_id(2)
    _flash_loop(q_ref[0, 0],
                lambda j: k_ref[0, 0, pl.ds(pl.multiple_of(j * tq, tq), tq), :],
                lambda j: vt_ref[0, 0, j],
                qi, tq, m_ref, l_ref, acc_ref)
    o_t = acc_ref[...] * (1.0 / l_ref[...])
    o_ref[0] = o_t.T.astype(BF16)


def _mla_attn(q, k, vt):
    B, H, S, Dk = q.shape
    tq = ATTN_TILE
    nkv = vt.shape[2]
    Dv = vt.shape[3]
    return pl.pallas_call(
        _mla_attn_kernel,
        grid=(B, H, S // tq),
        in_specs=[
            pl.BlockSpec((1, 1, tq, Dk), lambda b, h, i: (b, h, i, 0)),
            pl.BlockSpec((1, 1, S, Dk), lambda b, h, i: (b, h, 0, 0)),
            pl.BlockSpec((1, 1, nkv, Dv, tq), lambda b, h, i: (b, h, 0, 0, 0)),
        ],
        out_specs=pl.BlockSpec((1, tq, Dv), lambda b, h, i: (b, i, h)),
        out_shape=jax.ShapeDtypeStruct((B, S, H * Dv), BF16),
        scratch_shapes=[
            pltpu.VMEM((1, tq), F32), pltpu.VMEM((1, tq), F32), pltpu.VMEM((Dv, tq), F32),
        ],
        compiler_params=pltpu.CompilerParams(
            dimension_semantics=("parallel", "parallel", "arbitrary"), vmem_limit_bytes=VMEM_LIMIT),
        name="mla_attn",
    )(q, k, vt)


def _diff_proj_kernel(h_ref, cos_ref, sin_ref, wq_ref, wk_ref, wvt_ref, q_ref, k_ref, vt_ref, *, q_scale):
    hb = h_ref[0].astype(BF16)
    q = _dot(hb, wq_ref[...])
    k = _dot(hb, wk_ref[...])
    vt = _nt_dot(wvt_ref[...], hb)
    cos = cos_ref[...]
    sin = sin_ref[...]
    for h in range(DIFF_HEADS):
        lo, hi = h * LANES, (h + 1) * LANES
        q_ref[0, :, lo:hi] = (_rope_chunk(q[:, lo:hi], cos, sin) * q_scale).astype(BF16)
        k_ref[0, :, lo:hi] = _rope_chunk(k[:, lo:hi], cos, sin).astype(BF16)
        vt_ref[0, h, 0] = vt[lo:hi, :].astype(BF16)


def _diff_proj(h, cos, sin, wq, wk, wvt, q_scale):
    B, S, D = h.shape
    ts = TOKEN_TILE
    H = DIFF_HEADS
    W = H * LANES
    out_shape = (
        jax.ShapeDtypeStruct((B, S, W), BF16),
        jax.ShapeDtypeStruct((B, S, W), BF16),
        jax.ShapeDtypeStruct((B, H, S // ts, LANES, ts), BF16),
    )
    return pl.pallas_call(
        functools.partial(_diff_proj_kernel, q_scale=q_scale),
        grid=(B, S // ts),
        in_specs=[
            pl.BlockSpec((1, ts, D), lambda b, s: (b, s, 0)),
            pl.BlockSpec((ts, LANES), lambda b, s: (s, 0)),
            pl.BlockSpec((ts, LANES), lambda b, s: (s, 0)),
            _const_spec(wq.shape), _const_spec(wk.shape), _const_spec(wvt.shape),
        ],
        out_specs=(
            pl.BlockSpec((1, ts, W), lambda b, s: (b, s, 0)),
            pl.BlockSpec((1, ts, W), lambda b, s: (b, s, 0)),
            pl.BlockSpec((1, H, 1, LANES, ts), lambda b, s: (b, 0, s, 0, 0)),
        ),
        out_shape=out_shape,
        compiler_params=pltpu.CompilerParams(
            dimension_semantics=("parallel", "parallel"), vmem_limit_bytes=VMEM_LIMIT),
        name="diff_proj",
    )(h, cos, sin, wq, wk, wvt)


def _diff_attn_kernel(q_ref, k_ref, vt_ref, lq1_ref, lk1_ref, lq2_ref, lk2_ref, subln_ref, o_ref,
                      qcat_ref, m_ref, l_ref, acc_ref, *, lambda_init):
    tq = q_ref.shape[1]
    qi = pl.program_id(2)
    q = q_ref[0]
    lane = lax.broadcasted_iota(jnp.int32, q.shape, 1)
    zero = jnp.zeros_like(q)
    qcat_ref[0:tq, :] = jnp.where(lane < DIFF_HEAD_DIM, q, zero)
    qcat_ref[tq:2 * tq, :] = jnp.where(lane >= DIFF_HEAD_DIM, q, zero)
    _flash_loop(qcat_ref[...],
                lambda j: k_ref[0, pl.ds(pl.multiple_of(j * tq, tq), tq), :],
                lambda j: vt_ref[0, 0, j],
                qi, tq, m_ref, l_ref, acc_ref)
    n = acc_ref[...] * (1.0 / l_ref[...])
    lam = (jnp.exp(jnp.sum(lq1_ref[...] * lk1_ref[...], axis=-1, keepdims=True))
           - jnp.exp(jnp.sum(lq2_ref[...] * lk2_ref[...], axis=-1, keepdims=True)) + lambda_init)
    o = (n[:, :tq] - lam * n[:, tq:]).T
    o = _rms(o, subln_ref[...]) * (1.0 - lambda_init)
    o_ref[0] = o.astype(BF16)


def _diff_attn(q, k, vt, lq1, lk1, lq2, lk2, subln, lambda_init):
    B, S, W = q.shape
    H = DIFF_HEADS
    tq = ATTN_TILE
    nkv = vt.shape[2]
    Dv = vt.shape[3]
    small = lambda a: _const_spec(a.shape)
    return pl.pallas_call(
        functools.partial(_diff_attn_kernel, lambda_init=lambda_init),
        grid=(B, H, S // tq),
        in_specs=[
            pl.BlockSpec((1, tq, LANES), lambda b, h, i: (b, i, h)),
            pl.BlockSpec((1, S, LANES), lambda b, h, i: (b, 0, h)),
            pl.BlockSpec((1, 1, nkv, Dv, tq), lambda b, h, i: (b, h, 0, 0, 0)),
            small(lq1), small(lk1), small(lq2), small(lk2), small(subln),
        ],
        out_specs=pl.BlockSpec((1, tq, Dv), lambda b, h, i: (b, i, h)),
        out_shape=jax.ShapeDtypeStruct((B, S, H * Dv), BF16),
        scratch_shapes=[
            pltpu.VMEM((2 * tq, LANES), BF16),
            pltpu.VMEM((1, 2 * tq), F32), pltpu.VMEM((1, 2 * tq), F32), pltpu.VMEM((Dv, 2 * tq), F32),
        ],
        compiler_params=pltpu.CompilerParams(
            dimension_semantics=("parallel", "parallel", "arbitrary"), vmem_limit_bytes=VMEM_LIMIT),
        name="diff_attn",
    )(q, k, vt, lq1, lk1, lq2, lk2, subln)


def _post_kernel(x_ref, o_ref, wo_ref, g1_ref, b1_ref, wgu_ref, wd_ref, g2_ref, b2_ref, out_ref, *, d_ff):
    a = _dot(o_ref[...], wo_ref[...])
    h1 = _layernorm(DEEPNORM_ALPHA * x_ref[...] + a, g1_ref[...], b1_ref[...])
    hb = h1.astype(BF16)
    acc = jnp.zeros(h1.shape, F32)
    for lo in range(0, d_ff, FF_CHUNK):
        hi = min(lo + FF_CHUNK, d_ff)
        g = _dot(hb, wgu_ref[:, lo:hi])
        u = _dot(hb, wgu_ref[:, d_ff + lo:d_ff + hi])
        act = (g * jax.nn.sigmoid(g) * u).astype(BF16)
        acc = acc + _dot(act, wd_ref[lo:hi, :])
    out_ref[...] = _layernorm(DEEPNORM_ALPHA * h1 + acc, g2_ref[...], b2_ref[...])


def _post(x2d, o2d, wo, g1, b1, wgu, wd, g2, b2):
    T, D = x2d.shape
    ts = TOKEN_TILE
    d_ff = wd.shape[0]
    return pl.pallas_call(
        functools.partial(_post_kernel, d_ff=d_ff),
        grid=(T // ts,),
        in_specs=[
            pl.BlockSpec((ts, D), lambda t: (t, 0)),
            pl.BlockSpec((ts, o2d.shape[1]), lambda t: (t, 0)),
            _const_spec(wo.shape, True), _const_spec(g1.shape), _const_spec(b1.shape),
            _const_spec(wgu.shape, True), _const_spec(wd.shape, True),
            _const_spec(g2.shape), _const_spec(b2.shape),
        ],
        out_specs=pl.BlockSpec((ts, D), lambda t: (t, 0)),
        out_shape=jax.ShapeDtypeStruct((T, D), F32),
        compiler_params=pltpu.CompilerParams(
            dimension_semantics=("parallel",), vmem_limit_bytes=VMEM_LIMIT),
        name="post",
    )(x2d, o2d, wo, g1, b1, wgu, wd, g2, b2)


def _rope_tables(seq_len, dim):
    inv = 1.0 / (ROPE_THETA ** (jnp.arange(0, dim, 2, dtype=F32) / dim))
    ang = jnp.arange(seq_len, dtype=F32)[:, None] * inv[None, :]
    cos = jnp.cos(ang)
    sin = jnp.sin(ang)
    return jnp.concatenate([cos, cos], axis=-1), jnp.concatenate([-sin, sin], axis=-1)


def kernel(x, mla_w_dq, mla_q_norm, mla_w_uq, mla_w_dkv, mla_kv_norm, mla_w_ukv, mla_w_o, kv_w, diff_w_q, diff_lq1, diff_lk1, diff_lq2, diff_lk2, diff_subln, diff_w_o, ln1_g, ln1_b, ln2_g, ln2_b, ffn_w_gate_up, ffn_w_down):
    B, S, D = x.shape
    H = MLA_HEADS
    assert S % ATTN_TILE == 0 and S % TOKEN_TILE == 0 and TOKEN_TILE == ATTN_TILE
    assert MLA_ROPE == DIFF_HEAD_DIM == 64 and MLA_NOPE == MLA_V == LANES and 2 * DIFF_HEAD_DIM == LANES

    cos64, sin64 = _rope_tables(S, MLA_ROPE)
    zeros64 = jnp.zeros_like(cos64)
    cos_mla = jnp.concatenate([cos64, zeros64], axis=-1)
    sin_mla = jnp.concatenate([sin64, zeros64], axis=-1)
    cos_diff = jnp.concatenate([cos64, cos64], axis=-1)
    sin_diff = jnp.concatenate([sin64, sin64], axis=-1)

    row = lambda v: v.reshape(1, -1).astype(F32)

    w_dkv = mla_w_dkv[0]
    w_in = jnp.concatenate(
        [mla_w_dq[0], w_dkv[:, :MLA_KV_LORA], w_dkv[:, MLA_KV_LORA:],
         jnp.zeros((D, LANES - MLA_ROPE), F32)], axis=1).astype(BF16)
    w_uq = mla_w_uq[0].reshape(MLA_Q_LORA, H, MLA_NOPE + MLA_ROPE)
    w_uq_rope = jnp.pad(w_uq[:, :, MLA_NOPE:], ((0, 0), (0, 0), (0, LANES - MLA_ROPE)))
    w_uq2 = jnp.concatenate(
        [w_uq[:, :, :MLA_NOPE].reshape(MLA_Q_LORA, H * MLA_NOPE),
         w_uq_rope.reshape(MLA_Q_LORA, H * LANES)], axis=1).astype(BF16)
    w_ukv = mla_w_ukv[0].reshape(MLA_KV_LORA, H, MLA_NOPE + MLA_V)
    w_uk = w_ukv[:, :, :MLA_NOPE].reshape(MLA_KV_LORA, H * MLA_NOPE).astype(BF16)
    w_uvt = w_ukv[:, :, MLA_NOPE:].reshape(MLA_KV_LORA, H * MLA_V).T.astype(BF16)

    mla_scale = (MLA_NOPE + MLA_ROPE) ** -0.5 * LOG2E
    q, k, vt = _mla_proj(x, cos_mla, sin_mla, w_in, row(mla_q_norm[0]), row(mla_kv_norm[0]),
                         w_uq2, w_uk, w_uvt, mla_scale)
    o = _mla_attn(q, k, vt)

    x2d = x.reshape(B * S, D)
    h = _post(x2d, o.reshape(B * S, -1), mla_w_o[0].astype(BF16), row(ln1_g[0]), row(ln1_b[0]),
              ffn_w_gate_up[0].astype(BF16), ffn_w_down[0].astype(BF16), row(ln2_g[0]), row(ln2_b[0]))

    qk_width = DIFF_HEADS * 2 * DIFF_HEAD_DIM
    diff_scale = DIFF_HEAD_DIM ** -0.5 * LOG2E
    lambda_init = 0.8 - 0.6 * math.exp(-0.3 * 1)
    dq, dk, dvt = _diff_proj(h.reshape(B, S, D), cos_diff, sin_diff, diff_w_q[0].astype(BF16),
                             kv_w[:, :qk_width].astype(BF16), kv_w[:, qk_width:].T.astype(BF16), diff_scale)
    do = _diff_attn(dq, dk, dvt, row(diff_lq1[0]), row(diff_lk1[0]), row(diff_lq2[0]), row(diff_lk2[0]),
                    row(diff_subln[0]), lambda_init)
    out = _post(h, do.reshape(B * S, -1), diff_w_o[0].astype(BF16), row(ln1_g[1]), row(ln1_b[1]),
                ffn_w_gate_up[1].astype(BF16), ffn_w_down[1].astype(BF16), row(ln2_g[1]), row(ln2_b[1]))
    return out.reshape(B, S, D)
```

```python
import functools
import math

import jax
import jax.numpy as jnp
from jax import lax
from jax.experimental import pallas as pl
from jax.experimental.pallas import tpu as pltpu

F32 = jnp.float32
BF16 = jnp.bfloat16

MLA_HEADS = 8
MLA_NOPE = 128
MLA_ROPE = 64
MLA_V = 128
MLA_Q_LORA = 384
MLA_KV_LORA = 256
DIFF_HEADS = 8
DIFF_HEAD_DIM = 64
ROPE_THETA = 10000.0
DEPTH = 2
DEEPNORM_ALPHA = (2.0 * DEPTH) ** 0.25
LN_EPS = 1e-5
RMS_EPS = 1e-6
LOG2E = math.log2(math.e)
NEG_BIG = -1e30

LANES = 128
BF16_SUBLANES = 16
TOKEN_TILE = 512
ATTN_TILE = 512
MLA_HEADS_PER_STEP = 4
DIFF_HEADS_PER_STEP = 2
FF_CHUNK = 256
VMEM_LIMIT = 56 * 1024 * 1024
VT_ROWS = LANES + BF16_SUBLANES


def _nt_dot(a, b):
    return lax.dot_general(a, b, (((1,), (1,)), ((), ())), preferred_element_type=F32)


def _dot(a, b):
    return jnp.dot(a, b, preferred_element_type=F32)


def _rope_chunk(x, cos, sin_signed):
    lane = lax.broadcasted_iota(jnp.int32, x.shape, 1)
    first_half = (lane % 64) < 32
    rot = jnp.where(first_half, pltpu.roll(x, LANES - 32, 1), pltpu.roll(x, 32, 1))
    return x * cos + rot * sin_signed


def _rms(x, g):
    return x * lax.rsqrt(jnp.mean(jnp.square(x), axis=-1, keepdims=True) + RMS_EPS) * g


def _layernorm(x, g, b):
    mu = jnp.mean(x, axis=-1, keepdims=True)
    xc = x - mu
    var = jnp.mean(jnp.square(xc), axis=-1, keepdims=True)
    return xc * lax.rsqrt(var + LN_EPS) * g + b


def _const_spec(shape, single_buffer=False):
    nd = len(shape)
    kw = {"pipeline_mode": pl.Buffered(1)} if single_buffer else {}
    return pl.BlockSpec(shape, lambda *_: (0,) * nd, **kw)


def _store_vt(vt_ref, h, vt):
    ts = vt.shape[1]
    vt_ref[0, h, 0, 0:LANES, :] = vt.astype(BF16)
    row = lax.broadcasted_iota(jnp.int32, (BF16_SUBLANES, ts), 0)
    vt_ref[0, h, 0, LANES:VT_ROWS, :] = jnp.where(row == 0, 1.0, 0.0).astype(BF16)


def _mla_proj_kernel(x_ref, cos_ref, sin_ref, w_in_ref, qn_ref, kvn_ref, w_uq_ref, w_uk_ref, w_uvt_ref,
                     q_ref, k_ref, vt_ref, *, q_scale):
    xb = x_ref[0].astype(BF16)
    y = _dot(xb, w_in_ref[...])
    cq = _rms(y[:, :MLA_Q_LORA], qn_ref[...])
    c = _rms(y[:, MLA_Q_LORA:MLA_Q_LORA + MLA_KV_LORA], kvn_ref[...])
    cos = cos_ref[...]
    sin = sin_ref[...]
    k_rope = _rope_chunk(y[:, MLA_Q_LORA + MLA_KV_LORA:], cos, sin).astype(BF16)
    q = _dot(cq.astype(BF16), w_uq_ref[...])
    cb = c.astype(BF16)
    k_nope = _dot(cb, w_uk_ref[...])
    vt = _nt_dot(w_uvt_ref[...], cb)
    rope_base = MLA_HEADS * MLA_NOPE
    for h in range(MLA_HEADS):
        lo, hi = h * LANES, (h + 1) * LANES
        q_ref[0, h, :, 0:LANES] = (q[:, lo:hi] * q_scale).astype(BF16)
        q_ref[0, h, :, LANES:2 * LANES] = (
            _rope_chunk(q[:, rope_base + lo:rope_base + hi], cos, sin) * q_scale).astype(BF16)
        k_ref[0, h, :, 0:LANES] = k_nope[:, lo:hi].astype(BF16)
        k_ref[0, h, :, LANES:2 * LANES] = k_rope
        _store_vt(vt_ref, h, vt[lo:hi, :])


def _mla_proj(x, cos, sin, w_in, qn, kvn, w_uq, w_uk, w_uvt, q_scale):
    B, S, D = x.shape
    ts = TOKEN_TILE
    H = MLA_HEADS
    out_shape = (
        jax.ShapeDtypeStruct((B, H, S, 2 * LANES), BF16),
        jax.ShapeDtypeStruct((B, H, S, 2 * LANES), BF16),
        jax.ShapeDtypeStruct((B, H, S // ts, VT_ROWS, ts), BF16),
    )
    return pl.pallas_call(
        functools.partial(_mla_proj_kernel, q_scale=q_scale),
        grid=(B, S // ts),
        in_specs=[
            pl.BlockSpec((1, ts, D), lambda b, s: (b, s, 0)),
            pl.BlockSpec((ts, LANES), lambda b, s: (s, 0)),
            pl.BlockSpec((ts, LANES), lambda b, s: (s, 0)),
            _const_spec(w_in.shape), _const_spec(qn.shape), _const_spec(kvn.shape),
            _const_spec(w_uq.shape), _const_spec(w_uk.shape), _const_spec(w_uvt.shape),
        ],
        out_specs=(
            pl.BlockSpec((1, H, ts, 2 * LANES), lambda b, s: (b, 0, s, 0)),
            pl.BlockSpec((1, H, ts, 2 * LANES), lambda b, s: (b, 0, s, 0)),
            pl.BlockSpec((1, H, 1, VT_ROWS, ts), lambda b, s: (b, 0, s, 0, 0)),
        ),
        out_shape=out_shape,
        compiler_params=pltpu.CompilerParams(
            dimension_semantics=("parallel", "parallel"), vmem_limit_bytes=VMEM_LIMIT),
        name="mla_proj",
    )(x, cos, sin, w_in, qn, kvn, w_uq, w_uk, w_uvt)


def _flash_loop(streams, n_full, m_ref, acc_ref):
    m_ref[...] = jnp.full(m_ref.shape, NEG_BIG, F32)
    acc_ref[...] = jnp.zeros(acc_ref.shape, F32)

    def step(j, diagonal):
        scores = [_nt_dot(k_block(j), q) for q, k_block, _ in streams]
        for i, (_, _, vt_block) in enumerate(streams):
            s = scores[i]
            if diagonal:
                row = lax.broadcasted_iota(jnp.int32, s.shape, 0)
                col = lax.broadcasted_iota(jnp.int32, s.shape, 1)
                s = jnp.where(row <= col, s, NEG_BIG)
            m_prev = m_ref[i]
            m_new = jnp.maximum(m_prev, jnp.max(s, axis=0, keepdims=True))
            alpha = jnp.exp2(m_prev - m_new)
            p = jnp.exp2(s - m_new).astype(BF16)
            m_ref[i] = m_new
            acc_ref[i] = acc_ref[i] * alpha + _dot(vt_block(j), p)

    def body(j, carry):
        step(j, False)
        return carry

    lax.fori_loop(0, n_full, body, 0)
    step(n_full, True)


def _normalised(acc_ref, i):
    return acc_ref[i, 0:LANES, :] * (1.0 / acc_ref[i, LANES:LANES + 1, :])


def _mla_attn_kernel(q_ref, k_ref, vt_ref, o_ref, m_ref, acc_ref):
    hps, tq = q_ref.shape[1], q_ref.shape[2]
    qi = pl.program_id(2)
    streams = [(q_ref[0, h],
                lambda j, h=h: k_ref[0, h, pl.ds(pl.multiple_of(j * tq, tq), tq), :],
                lambda j, h=h: vt_ref[0, h, j]) for h in range(hps)]
    _flash_loop(streams, qi, m_ref, acc_ref)
    for h in range(hps):
        o_ref[0, :, h * LANES:(h + 1) * LANES] = _normalised(acc_ref, h).T.astype(BF16)


def _mla_attn(q, k, vt):
    B, H, S, Dk = q.shape
    tq = ATTN_TILE
    hps = MLA_HEADS_PER_STEP
    nkv = vt.shape[2]
    return pl.pallas_call(
        _mla_attn_kernel,
        grid=(B, H // hps, S // tq),
        in_specs=[
            pl.BlockSpec((1, hps, tq, Dk), lambda b, h, i: (b, h, i, 0)),
            pl.BlockSpec((1, hps, S, Dk), lambda b, h, i: (b, h, 0, 0)),
            pl.BlockSpec((1, hps, nkv, VT_ROWS, tq), lambda b, h, i: (b, h, 0, 0, 0)),
        ],
        out_specs=pl.BlockSpec((1, tq, hps * LANES), lambda b, h, i: (b, i, h)),
        out_shape=jax.ShapeDtypeStruct((B, S, H * LANES), BF16),
        scratch_shapes=[pltpu.VMEM((hps, 1, tq), F32), pltpu.VMEM((hps, VT_ROWS, tq), F32)],
        compiler_params=pltpu.CompilerParams(
            dimension_semantics=("parallel", "parallel", "arbitrary"), vmem_limit_bytes=VMEM_LIMIT),
        name="mla_attn",
    )(q, k, vt)


def _diff_proj_kernel(h_ref, cos_ref, sin_ref, wq_ref, wk_ref, wvt_ref, q_ref, k_ref, vt_ref, *, q_scale):
    hb = h_ref[0].astype(BF16)
    q = _dot(hb, wq_ref[...])
    k = _dot(hb, wk_ref[...])
    vt = _nt_dot(wvt_ref[...], hb)
    cos = cos_ref[...]
    sin = sin_ref[...]
    for h in range(DIFF_HEADS):
        lo, hi = h * LANES, (h + 1) * LANES
        q_ref[0, :, lo:hi] = (_rope_chunk(q[:, lo:hi], cos, sin) * q_scale).astype(BF16)
        k_ref[0, :, lo:hi] = _rope_chunk(k[:, lo:hi], cos, sin).astype(BF16)
        _store_vt(vt_ref, h, vt[lo:hi, :])


def _diff_proj(h, cos, sin, wq, wk, wvt, q_scale):
    B, S, D = h.shape
    ts = TOKEN_TILE
    H = DIFF_HEADS
    W = H * LANES
    out_shape = (
        jax.ShapeDtypeStruct((B, S, W), BF16),
        jax.ShapeDtypeStruct((B, S, W), BF16),
        jax.ShapeDtypeStruct((B, H, S // ts, VT_ROWS, ts), BF16),
    )
    return pl.pallas_call(
        functools.partial(_diff_proj_kernel, q_scale=q_scale),
        grid=(B, S // ts),
        in_specs=[
            pl.BlockSpec((1, ts, D), lambda b, s: (b, s, 0)),
            pl.BlockSpec((ts, LANES), lambda b, s: (s, 0)),
            pl.BlockSpec((ts, LANES), lambda b, s: (s, 0)),
            _const_spec(wq.shape), _const_spec(wk.shape), _const_spec(wvt.shape),
        ],
        out_specs=(
            pl.BlockSpec((1, ts, W), lambda b, s: (b, s, 0)),
            pl.BlockSpec((1, ts, W), lambda b, s: (b, s, 0)),
            pl.BlockSpec((1, H, 1, VT_ROWS, ts), lambda b, s: (b, 0, s, 0, 0)),
        ),
        out_shape=out_shape,
        compiler_params=pltpu.CompilerParams(
            dimension_semantics=("parallel", "parallel"), vmem_limit_bytes=VMEM_LIMIT),
        name="diff_proj",
    )(h, cos, sin, wq, wk, wvt)


def _diff_attn_kernel(q_ref, k_ref, vt_ref, lq1_ref, lk1_ref, lq2_ref, lk2_ref, subln_ref, o_ref,
                      qmap_ref, m_ref, acc_ref, *, lambda_init):
    tq = q_ref.shape[1]
    hps = vt_ref.shape[1]
    qi = pl.program_id(2)
    lane = lax.broadcasted_iota(jnp.int32, (tq, LANES), 1)
    for h in range(hps):
        q = q_ref[0, :, h * LANES:(h + 1) * LANES]
        zero = jnp.zeros_like(q)
        qmap_ref[2 * h] = jnp.where(lane < DIFF_HEAD_DIM, q, zero)
        qmap_ref[2 * h + 1] = jnp.where(lane >= DIFF_HEAD_DIM, q, zero)
    streams = [(qmap_ref[i],
                lambda j, h=i // 2: k_ref[0, pl.ds(pl.multiple_of(j * tq, tq), tq), h * LANES:(h + 1) * LANES],
                lambda j, h=i // 2: vt_ref[0, h, j]) for i in range(2 * hps)]
    _flash_loop(streams, qi, m_ref, acc_ref)
    lam = (jnp.exp(jnp.sum(lq1_ref[...] * lk1_ref[...], axis=-1, keepdims=True))
           - jnp.exp(jnp.sum(lq2_ref[...] * lk2_ref[...], axis=-1, keepdims=True)) + lambda_init)
    for h in range(hps):
        o = (_normalised(acc_ref, 2 * h) - lam * _normalised(acc_ref, 2 * h + 1)).T
        o = _rms(o, subln_ref[...]) * (1.0 - lambda_init)
        o_ref[0, :, h * LANES:(h + 1) * LANES] = o.astype(BF16)


def _diff_attn(q, k, vt, lq1, lk1, lq2, lk2, subln, lambda_init):
    B, S, W = q.shape
    H = DIFF_HEADS
    tq = ATTN_TILE
    hps = DIFF_HEADS_PER_STEP
    nkv = vt.shape[2]
    small = lambda a: _const_spec(a.shape)
    return pl.pallas_call(
        functools.partial(_diff_attn_kernel, lambda_init=lambda_init),
        grid=(B, H // hps, S // tq),
        in_specs=[
            pl.BlockSpec((1, tq, hps * LANES), lambda b, h, i: (b, i, h)),
            pl.BlockSpec((1, S, hps * LANES), lambda b, h, i: (b, 0, h)),
            pl.BlockSpec((1, hps, nkv, VT_ROWS, tq), lambda b, h, i: (b, h, 0, 0, 0)),
            small(lq1), small(lk1), small(lq2), small(lk2), small(subln),
        ],
        out_specs=pl.BlockSpec((1, tq, hps * LANES), lambda b, h, i: (b, i, h)),
        out_shape=jax.ShapeDtypeStruct((B, S, H * LANES), BF16),
        scratch_shapes=[
            pltpu.VMEM((2 * hps, tq, LANES), BF16),
            pltpu.VMEM((2 * hps, 1, tq), F32),
            pltpu.VMEM((2 * hps, VT_ROWS, tq), F32),
        ],
        compiler_params=pltpu.CompilerParams(
            dimension_semantics=("parallel", "parallel", "arbitrary"), vmem_limit_bytes=VMEM_LIMIT),
        name="diff_attn",
    )(q, k, vt, lq1, lk1, lq2, lk2, subln)


def _post_kernel(x_ref, o_ref, wo_ref, g1_ref, b1_ref, wgu_ref, wd_ref, g2_ref, b2_ref, out_ref, *, d_ff):
    a = _dot(o_ref[...], wo_ref[...])
    h1 = _layernorm(DEEPNORM_ALPHA * x_ref[...] + a, g1_ref[...], b1_ref[...])
    hb = h1.astype(BF16)
    acc = jnp.zeros(h1.shape, F32)
    for lo in range(0, d_ff, FF_CHUNK):
        hi = min(lo + FF_CHUNK, d_ff)
        g = _dot(hb, wgu_ref[:, lo:hi])
        u = _dot(hb, wgu_ref[:, d_ff + lo:d_ff + hi])
        act = (g * jax.nn.sigmoid(g) * u).astype(BF16)
        acc = acc + _dot(act, wd_ref[lo:hi, :])
    out_ref[...] = _layernorm(DEEPNORM_ALPHA * h1 + acc, g2_ref[...], b2_ref[...])


def _post(x2d, o2d, wo, g1, b1, wgu, wd, g2, b2):
    T, D = x2d.shape
    ts = TOKEN_TILE
    d_ff = wd.shape[0]
    return pl.pallas_call(
        functools.partial(_post_kernel, d_ff=d_ff),
        grid=(T // ts,),
        in_specs=[
            pl.BlockSpec((ts, D), lambda t: (t, 0)),
            pl.BlockSpec((ts, o2d.shape[1]), lambda t: (t, 0)),
            _const_spec(wo.shape, True), _const_spec(g1.shape), _const_spec(b1.shape),
            _const_spec(wgu.shape, True), _const_spec(wd.shape, True),
            _const_spec(g2.shape), _const_spec(b2.shape),
        ],
        out_specs=pl.BlockSpec((ts, D), lambda t: (t, 0)),
        out_shape=jax.ShapeDtypeStruct((T, D), F32),
        compiler_params=pltpu.CompilerParams(
            dimension_semantics=("parallel",), vmem_limit_bytes=VMEM_LIMIT),
        name="post",
    )(x2d, o2d, wo, g1, b1, wgu, wd, g2, b2)


def _rope_tables(seq_len, dim):
    inv = 1.0 / (ROPE_THETA ** (jnp.arange(0, dim, 2, dtype=F32) / dim))
    ang = jnp.arange(seq_len, dtype=F32)[:, None] * inv[None, :]
    cos = jnp.cos(ang)
    sin = jnp.sin(ang)
    return jnp.concatenate([cos, cos], axis=-1), jnp.concatenate([-sin, sin], axis=-1)


def kernel(x, mla_w_dq, mla_q_norm, mla_w_uq, mla_w_dkv, mla_kv_norm, mla_w_ukv, mla_w_o, kv_w, diff_w_q, diff_lq1, diff_lk1, diff_lq2, diff_lk2, diff_subln, diff_w_o, ln1_g, ln1_b, ln2_g, ln2_b, ffn_w_gate_up, ffn_w_down):
    B, S, D = x.shape
    H = MLA_HEADS
    assert S % ATTN_TILE == 0 and S % TOKEN_TILE == 0 and TOKEN_TILE == ATTN_TILE
    assert MLA_ROPE == DIFF_HEAD_DIM == 64 and MLA_NOPE == MLA_V == LANES and 2 * DIFF_HEAD_DIM == LANES
    assert MLA_HEADS % MLA_HEADS_PER_STEP == 0 and DIFF_HEADS % DIFF_HEADS_PER_STEP == 0

    cos64, sin64 = _rope_tables(S, MLA_ROPE)
    zeros64 = jnp.zeros_like(cos64)
    cos_mla = jnp.concatenate([cos64, zeros64], axis=-1)
    sin_mla = jnp.concatenate([sin64, zeros64], axis=-1)
    cos_diff = jnp.concatenate([cos64, cos64], axis=-1)
    sin_diff = jnp.concatenate([sin64, sin64], axis=-1)

    row = lambda v: v.reshape(1, -1).astype(F32)

    w_dkv = mla_w_dkv[0]
    w_in = jnp.concatenate(
        [mla_w_dq[0], w_dkv[:, :MLA_KV_LORA], w_dkv[:, MLA_KV_LORA:],
         jnp.zeros((D, LANES - MLA_ROPE), F32)], axis=1).astype(BF16)
    w_uq = mla_w_uq[0].reshape(MLA_Q_LORA, H, MLA_NOPE + MLA_ROPE)
    w_uq_rope = jnp.pad(w_uq[:, :, MLA_NOPE:], ((0, 0), (0, 0), (0, LANES - MLA_ROPE)))
    w_uq2 = jnp.concatenate(
        [w_uq[:, :, :MLA_NOPE].reshape(MLA_Q_LORA, H * MLA_NOPE),
         w_uq_rope.reshape(MLA_Q_LORA, H * LANES)], axis=1).astype(BF16)
    w_ukv = mla_w_ukv[0].reshape(MLA_KV_LORA, H, MLA_NOPE + MLA_V)
    w_uk = w_ukv[:, :, :MLA_NOPE].reshape(MLA_KV_LORA, H * MLA_NOPE).astype(BF16)
    w_uvt = w_ukv[:, :, MLA_NOPE:].reshape(MLA_KV_LORA, H * MLA_V).T.astype(BF16)

    mla_scale = (MLA_NOPE + MLA_ROPE) ** -0.5 * LOG2E
    q, k, vt = _mla_proj(x, cos_mla, sin_mla, w_in, row(mla_q_norm[0]), row(mla_kv_norm[0]),
                         w_uq2, w_uk, w_uvt, mla_scale)
    o = _mla_attn(q, k, vt)

    x2d = x.reshape(B * S, D)
    h = _post(x2d, o.reshape(B * S, -1), mla_w_o[0].astype(BF16), row(ln1_g[0]), row(ln1_b[0]),
              ffn_w_gate_up[0].astype(BF16), ffn_w_down[0].astype(BF16), row(ln2_g[0]), row(ln2_b[0]))

    qk_width = DIFF_HEADS * 2 * DIFF_HEAD_DIM
    diff_scale = DIFF_HEAD_DIM ** -0.5 * LOG2E
    lambda_init = 0.8 - 0.6 * math.exp(-0.3 * 1)
    dq, dk, dvt = _diff_proj(h.reshape(B, S, D), cos_diff, sin_diff, diff_w_q[0].astype(BF16),
                             kv_w[:, :qk_width].astype(BF16), kv_w[:, qk_width:].T.astype(BF16), diff_scale)
    do = _diff_attn(dq, dk, dvt, row(diff_lq1[0]), row(diff_lk1[0]), row(diff_lq2[0]), row(diff_lk2[0]),
                    row(diff_subln[0]), lambda_init)
    out = _post(h, do.reshape(B * S, -1), diff_w_o[0].astype(BF16), row(ln1_g[1]), row(ln1_b[1]),
                ffn_w_gate_up[1].astype(BF16), ffn_w_down[1].astype(BF16), row(ln2_g[1]), row(ln2_b[1]))
    return out.reshape(B, S, D)
```

```python
import functools
import math

import jax
import jax.numpy as jnp
from jax import lax
from jax.experimental import pallas as pl
from jax.experimental.pallas import tpu as pltpu

F32 = jnp.float32
BF16 = jnp.bfloat16

MLA_HEADS = 8
MLA_NOPE = 128
MLA_ROPE = 64
MLA_V = 128
MLA_Q_LORA = 384
MLA_KV_LORA = 256
DIFF_HEADS = 8
DIFF_HEAD_DIM = 64
ROPE_THETA = 10000.0
DEPTH = 2
DEEPNORM_ALPHA = (2.0 * DEPTH) ** 0.25
LN_EPS = 1e-5
RMS_EPS = 1e-6
LOG2E = math.log2(math.e)
NEG_BIG = -1e30

LANES = 128
BF16_SUBLANES = 16
TOKEN_TILE = 512
ATTN_TILE = 512
MLA_HEADS_PER_STEP = 4
DIFF_HEADS_PER_STEP = 4
FF_CHUNK = 256
VMEM_LIMIT = 56 * 1024 * 1024
VT_ROWS = LANES + BF16_SUBLANES


def _nt_dot(a, b):
    return lax.dot_general(a, b, (((1,), (1,)), ((), ())), preferred_element_type=F32)


def _dot(a, b):
    return jnp.dot(a, b, preferred_element_type=F32)


def _rope_chunk(x, cos, sin_signed):
    lane = lax.broadcasted_iota(jnp.int32, x.shape, 1)
    first_half = (lane % 64) < 32
    rot = jnp.where(first_half, pltpu.roll(x, LANES - 32, 1), pltpu.roll(x, 32, 1))
    return x * cos + rot * sin_signed


def _rms(x, g):
    return x * lax.rsqrt(jnp.mean(jnp.square(x), axis=-1, keepdims=True) + RMS_EPS) * g


def _layernorm(x, g, b):
    mu = jnp.mean(x, axis=-1, keepdims=True)
    xc = x - mu
    var = jnp.mean(jnp.square(xc), axis=-1, keepdims=True)
    return xc * lax.rsqrt(var + LN_EPS) * g + b


def _const_spec(shape, single_buffer=False):
    nd = len(shape)
    kw = {"pipeline_mode": pl.Buffered(1)} if single_buffer else {}
    return pl.BlockSpec(shape, lambda *_: (0,) * nd, **kw)


def _store_vt(vt_ref, h, vt):
    ts = vt.shape[1]
    vt_ref[0, h, 0, 0:LANES, :] = vt.astype(BF16)
    row = lax.broadcasted_iota(jnp.int32, (BF16_SUBLANES, ts), 0)
    vt_ref[0, h, 0, LANES:VT_ROWS, :] = jnp.where(row == 0, 1.0, 0.0).astype(BF16)


def _mla_proj_kernel(x_ref, cos_ref, sin_ref, w_in_ref, qn_ref, kvn_ref, w_uq_ref, w_uk_ref, w_uvt_ref,
                     q_ref, k_ref, vt_ref, *, q_scale):
    xb = x_ref[0].astype(BF16)
    y = _dot(xb, w_in_ref[...])
    cq = _rms(y[:, :MLA_Q_LORA], qn_ref[...])
    c = _rms(y[:, MLA_Q_LORA:MLA_Q_LORA + MLA_KV_LORA], kvn_ref[...])
    cos = cos_ref[...]
    sin = sin_ref[...]
    k_rope = _rope_chunk(y[:, MLA_Q_LORA + MLA_KV_LORA:], cos, sin).astype(BF16)
    q = _dot(cq.astype(BF16), w_uq_ref[...])
    cb = c.astype(BF16)
    k_nope = _dot(cb, w_uk_ref[...])
    vt = _nt_dot(w_uvt_ref[...], cb)
    rope_base = MLA_HEADS * MLA_NOPE
    for h in range(MLA_HEADS):
        lo, hi = h * LANES, (h + 1) * LANES
        q_ref[0, h, :, 0:LANES] = (q[:, lo:hi] * q_scale).astype(BF16)
        q_ref[0, h, :, LANES:2 * LANES] = (
            _rope_chunk(q[:, rope_base + lo:rope_base + hi], cos, sin) * q_scale).astype(BF16)
        k_ref[0, h, :, 0:LANES] = k_nope[:, lo:hi].astype(BF16)
        k_ref[0, h, :, LANES:2 * LANES] = k_rope
        _store_vt(vt_ref, h, vt[lo:hi, :])


def _mla_proj(x, cos, sin, w_in, qn, kvn, w_uq, w_uk, w_uvt, q_scale):
    B, S, D = x.shape
    ts = TOKEN_TILE
    H = MLA_HEADS
    out_shape = (
        jax.ShapeDtypeStruct((B, H, S, 2 * LANES), BF16),
        jax.ShapeDtypeStruct((B, H, S, 2 * LANES), BF16),
        jax.ShapeDtypeStruct((B, H, S // ts, VT_ROWS, ts), BF16),
    )
    return pl.pallas_call(
        functools.partial(_mla_proj_kernel, q_scale=q_scale),
        grid=(B, S // ts),
        in_specs=[
            pl.BlockSpec((1, ts, D), lambda b, s: (b, s, 0)),
            pl.BlockSpec((ts, LANES), lambda b, s: (s, 0)),
            pl.BlockSpec((ts, LANES), lambda b, s: (s, 0)),
            _const_spec(w_in.shape), _const_spec(qn.shape), _const_spec(kvn.shape),
            _const_spec(w_uq.shape), _const_spec(w_uk.shape), _const_spec(w_uvt.shape),
        ],
        out_specs=(
            pl.BlockSpec((1, H, ts, 2 * LANES), lambda b, s: (b, 0, s, 0)),
            pl.BlockSpec((1, H, ts, 2 * LANES), lambda b, s: (b, 0, s, 0)),
            pl.BlockSpec((1, H, 1, VT_ROWS, ts), lambda b, s: (b, 0, s, 0, 0)),
        ),
        out_shape=out_shape,
        compiler_params=pltpu.CompilerParams(
            dimension_semantics=("parallel", "parallel"), vmem_limit_bytes=VMEM_LIMIT),
        name="mla_proj",
    )(x, cos, sin, w_in, qn, kvn, w_uq, w_uk, w_uvt)


def _flash_loop(streams, qi, s_ref, m_ref, acc_ref):
    @pl.when(qi == 0)
    def _seed():
        for i, (q_cur, _, k_block, _) in enumerate(streams):
            s_ref[i] = _nt_dot(k_block(0), q_cur())

    m_ref[...] = jnp.full(m_ref.shape, NEG_BIG, F32)
    acc_ref[...] = jnp.zeros(acc_ref.shape, F32)

    def step(j, j_produce, use_next_tile, diagonal):
        for i, (q_cur, q_next, k_block, vt_block) in enumerate(streams):
            s = s_ref[i]
            if diagonal:
                row = lax.broadcasted_iota(jnp.int32, s.shape, 0)
                col = lax.broadcasted_iota(jnp.int32, s.shape, 1)
                s = jnp.where(row <= col, s, NEG_BIG)
            m_prev = m_ref[i]
            m_new = jnp.maximum(m_prev, jnp.max(s, axis=0, keepdims=True))
            alpha = jnp.exp2(m_prev - m_new)
            p = jnp.exp2(s - m_new).astype(BF16)
            m_ref[i] = m_new
            s_ref[i] = _nt_dot(k_block(j_produce), q_next() if use_next_tile else q_cur())
            acc_ref[i] = acc_ref[i] * alpha + _dot(vt_block(j), p)

    def body(j, carry):
        step(j, j + 1, False, False)
        return carry

    lax.fori_loop(0, qi, body, 0)
    step(qi, 0, True, True)


def _normalised(acc_ref, i):
    return acc_ref[i, 0:LANES, :] * (1.0 / acc_ref[i, LANES:LANES + 1, :])


def _mla_attn_kernel(q_ref, qn_ref, k_ref, vt_ref, o_ref, s_ref, m_ref, acc_ref):
    hps, tq = q_ref.shape[1], q_ref.shape[2]
    qi = pl.program_id(2)
    streams = [(lambda h=h: q_ref[0, h],
                lambda h=h: qn_ref[0, h],
                lambda j, h=h: k_ref[0, h, pl.ds(pl.multiple_of(j * tq, tq), tq), :],
                lambda j, h=h: vt_ref[0, h, j]) for h in range(hps)]
    _flash_loop(streams, qi, s_ref, m_ref, acc_ref)
    for h in range(hps):
        o_ref[0, :, h * LANES:(h + 1) * LANES] = _normalised(acc_ref, h).T.astype(BF16)


def _mla_attn(q, k, vt):
    B, H, S, Dk = q.shape
    tq = ATTN_TILE
    hps = MLA_HEADS_PER_STEP
    nkv = vt.shape[2]
    nq = S // tq
    return pl.pallas_call(
        _mla_attn_kernel,
        grid=(B, H // hps, nq),
        in_specs=[
            pl.BlockSpec((1, hps, tq, Dk), lambda b, h, i: (b, h, i, 0)),
            pl.BlockSpec((1, hps, tq, Dk), lambda b, h, i: (b, h, jnp.minimum(i + 1, nq - 1), 0)),
            pl.BlockSpec((1, hps, S, Dk), lambda b, h, i: (b, h, 0, 0)),
            pl.BlockSpec((1, hps, nkv, VT_ROWS, tq), lambda b, h, i: (b, h, 0, 0, 0)),
        ],
        out_specs=pl.BlockSpec((1, tq, hps * LANES), lambda b, h, i: (b, i, h)),
        out_shape=jax.ShapeDtypeStruct((B, S, H * LANES), BF16),
        scratch_shapes=[pltpu.VMEM((hps, tq, tq), F32), pltpu.VMEM((hps, 1, tq), F32),
                        pltpu.VMEM((hps, VT_ROWS, tq), F32)],
        compiler_params=pltpu.CompilerParams(
            dimension_semantics=("parallel", "parallel", "arbitrary"), vmem_limit_bytes=VMEM_LIMIT),
        name="mla_attn",
    )(q, q, k, vt)


def _diff_proj_kernel(h_ref, cos_ref, sin_ref, wq_ref, wk_ref, wvt_ref, q_ref, k_ref, vt_ref, *, q_scale):
    hb = h_ref[0].astype(BF16)
    q = _dot(hb, wq_ref[...])
    k = _dot(hb, wk_ref[...])
    vt = _nt_dot(wvt_ref[...], hb)
    cos = cos_ref[...]
    sin = sin_ref[...]
    for h in range(DIFF_HEADS):
        lo, hi = h * LANES, (h + 1) * LANES
        q_ref[0, :, lo:hi] = (_rope_chunk(q[:, lo:hi], cos, sin) * q_scale).astype(BF16)
        k_ref[0, :, lo:hi] = _rope_chunk(k[:, lo:hi], cos, sin).astype(BF16)
        _store_vt(vt_ref, h, vt[lo:hi, :])


def _diff_proj(h, cos, sin, wq, wk, wvt, q_scale):
    B, S, D = h.shape
    ts = TOKEN_TILE
    H = DIFF_HEADS
    W = H * LANES
    out_shape = (
        jax.ShapeDtypeStruct((B, S, W), BF16),
        jax.ShapeDtypeStruct((B, S, W), BF16),
        jax.ShapeDtypeStruct((B, H, S // ts, VT_ROWS, ts), BF16),
    )
    return pl.pallas_call(
        functools.partial(_diff_proj_kernel, q_scale=q_scale),
        grid=(B, S // ts),
        in_specs=[
            pl.BlockSpec((1, ts, D), lambda b, s: (b, s, 0)),
            pl.BlockSpec((ts, LANES), lambda b, s: (s, 0)),
            pl.BlockSpec((ts, LANES), lambda b, s: (s, 0)),
            _const_spec(wq.shape), _const_spec(wk.shape), _const_spec(wvt.shape),
        ],
        out_specs=(
            pl.BlockSpec((1, ts, W), lambda b, s: (b, s, 0)),
            pl.BlockSpec((1, ts, W), lambda b, s: (b, s, 0)),
            pl.BlockSpec((1, H, 1, VT_ROWS, ts), lambda b, s: (b, 0, s, 0, 0)),
        ),
        out_shape=out_shape,
        compiler_params=pltpu.CompilerParams(
            dimension_semantics=("parallel", "parallel"), vmem_limit_bytes=VMEM_LIMIT),
        name="diff_proj",
    )(h, cos, sin, wq, wk, wvt)


def _diff_attn_kernel(q_ref, qn_ref, k_ref, vt_ref, lq1_ref, lk1_ref, lq2_ref, lk2_ref, subln_ref, o_ref,
                      qmap_ref, s_ref, m_ref, acc_ref, *, lambda_init):
    tq = q_ref.shape[1]
    hps = vt_ref.shape[1]
    qi = pl.program_id(2)
    lane = lax.broadcasted_iota(jnp.int32, (tq, LANES), 1)

    def one_map(ref, h, c):
        q = ref[0, :, h * LANES:(h + 1) * LANES]
        keep = (lane < DIFF_HEAD_DIM) if c == 0 else (lane >= DIFF_HEAD_DIM)
        return jnp.where(keep, q, jnp.zeros_like(q))

    for i in range(2 * hps):
        qmap_ref[i] = one_map(q_ref, i // 2, i % 2)
    streams = [(lambda i=i: qmap_ref[i],
                lambda i=i: one_map(qn_ref, i // 2, i % 2),
                lambda j, h=i // 2: k_ref[0, pl.ds(pl.multiple_of(j * tq, tq), tq), h * LANES:(h + 1) * LANES],
                lambda j, h=i // 2: vt_ref[0, h, j]) for i in range(2 * hps)]
    _flash_loop(streams, qi, s_ref, m_ref, acc_ref)
    lam = (jnp.exp(jnp.sum(lq1_ref[...] * lk1_ref[...], axis=-1, keepdims=True))
           - jnp.exp(jnp.sum(lq2_ref[...] * lk2_ref[...], axis=-1, keepdims=True)) + lambda_init)
    for h in range(hps):
        o = (_normalised(acc_ref, 2 * h) - lam * _normalised(acc_ref, 2 * h + 1)).T
        o = _rms(o, subln_ref[...]) * (1.0 - lambda_init)
        o_ref[0, :, h * LANES:(h + 1) * LANES] = o.astype(BF16)


def _diff_attn(q, k, vt, lq1, lk1, lq2, lk2, subln, lambda_init):
    B, S, W = q.shape
    H = DIFF_HEADS
    tq = ATTN_TILE
    hps = DIFF_HEADS_PER_STEP
    nkv = vt.shape[2]
    nq = S // tq
    small = lambda a: _const_spec(a.shape)
    return pl.pallas_call(
        functools.partial(_diff_attn_kernel, lambda_init=lambda_init),
        grid=(B, H // hps, nq),
        in_specs=[
            pl.BlockSpec((1, tq, hps * LANES), lambda b, h, i: (b, i, h)),
            pl.BlockSpec((1, tq, hps * LANES), lambda b, h, i: (b, jnp.minimum(i + 1, nq - 1), h)),
            pl.BlockSpec((1, S, hps * LANES), lambda b, h, i: (b, 0, h)),
            pl.BlockSpec((1, hps, nkv, VT_ROWS, tq), lambda b, h, i: (b, h, 0, 0, 0)),
            small(lq1), small(lk1), small(lq2), small(lk2), small(subln),
        ],
        out_specs=pl.BlockSpec((1, tq, hps * LANES), lambda b, h, i: (b, i, h)),
        out_shape=jax.ShapeDtypeStruct((B, S, H * LANES), BF16),
        scratch_shapes=[
            pltpu.VMEM((2 * hps, tq, LANES), BF16),
            pltpu.VMEM((2 * hps, tq, tq), F32),
            pltpu.VMEM((2 * hps, 1, tq), F32),
            pltpu.VMEM((2 * hps, VT_ROWS, tq), F32),
        ],
        compiler_params=pltpu.CompilerParams(
            dimension_semantics=("parallel", "parallel", "arbitrary"), vmem_limit_bytes=VMEM_LIMIT),
        name="diff_attn",
    )(q, q, k, vt, lq1, lk1, lq2, lk2, subln)


def _post_kernel(x_ref, o_ref, wo_ref, g1_ref, b1_ref, wgu_ref, wd_ref, g2_ref, b2_ref, out_ref, *, d_ff):
    a = _dot(o_ref[...], wo_ref[...])
    h1 = _layernorm(DEEPNORM_ALPHA * x_ref[...] + a, g1_ref[...], b1_ref[...])
    hb = h1.astype(BF16)
    acc = jnp.zeros(h1.shape, F32)
    for lo in range(0, d_ff, FF_CHUNK):
        hi = min(lo + FF_CHUNK, d_ff)
        g = _dot(hb, wgu_ref[:, lo:hi])
        u = _dot(hb, wgu_ref[:, d_ff + lo:d_ff + hi])
        act = (g * jax.nn.sigmoid(g) * u).astype(BF16)
        acc = acc + _dot(act, wd_ref[lo:hi, :])
    out_ref[...] = _layernorm(DEEPNORM_ALPHA * h1 + acc, g2_ref[...], b2_ref[...])


def _post(x2d, o2d, wo, g1, b1, wgu, wd, g2, b2):
    T, D = x2d.shape
    ts = TOKEN_TILE
    d_ff = wd.shape[0]
    return pl.pallas_call(
        functools.partial(_post_kernel, d_ff=d_ff),
        grid=(T // ts,),
        in_specs=[
            pl.BlockSpec((ts, D), lambda t: (t, 0)),
            pl.BlockSpec((ts, o2d.shape[1]), lambda t: (t, 0)),
            _const_spec(wo.shape, True), _const_spec(g1.shape), _const_spec(b1.shape),
            _const_spec(wgu.shape, True), _const_spec(wd.shape, True),
            _const_spec(g2.shape), _const_spec(b2.shape),
        ],
        out_specs=pl.BlockSpec((ts, D), lambda t: (t, 0)),
        out_shape=jax.ShapeDtypeStruct((T, D), F32),
        compiler_params=pltpu.CompilerParams(
            dimension_semantics=("parallel",), vmem_limit_bytes=VMEM_LIMIT),
        name="post",
    )(x2d, o2d, wo, g1, b1, wgu, wd, g2, b2)


def _rope_tables(seq_len, dim):
    inv = 1.0 / (ROPE_THETA ** (jnp.arange(0, dim, 2, dtype=F32) / dim))
    ang = jnp.arange(seq_len, dtype=F32)[:, None] * inv[None, :]
    cos = jnp.cos(ang)
    sin = jnp.sin(ang)
    return jnp.concatenate([cos, cos], axis=-1), jnp.concatenate([-sin, sin], axis=-1)


def kernel(x, mla_w_dq, mla_q_norm, mla_w_uq, mla_w_dkv, mla_kv_norm, mla_w_ukv, mla_w_o, kv_w, diff_w_q, diff_lq1, diff_lk1, diff_lq2, diff_lk2, diff_subln, diff_w_o, ln1_g, ln1_b, ln2_g, ln2_b, ffn_w_gate_up, ffn_w_down):
    B, S, D = x.shape
    H = MLA_HEADS
    assert S % ATTN_TILE == 0 and S % TOKEN_TILE == 0 and TOKEN_TILE == ATTN_TILE
    assert MLA_ROPE == DIFF_HEAD_DIM == 64 and MLA_NOPE == MLA_V == LANES and 2 * DIFF_HEAD_DIM == LANES
    assert MLA_HEADS % MLA_HEADS_PER_STEP == 0 and DIFF_HEADS % DIFF_HEADS_PER_STEP == 0

    cos64, sin64 = _rope_tables(S, MLA_ROPE)
    zeros64 = jnp.zeros_like(cos64)
    cos_mla = jnp.concatenate([cos64, zeros64], axis=-1)
    sin_mla = jnp.concatenate([sin64, zeros64], axis=-1)
    cos_diff = jnp.concatenate([cos64, cos64], axis=-1)
    sin_diff = jnp.concatenate([sin64, sin64], axis=-1)

    row = lambda v: v.reshape(1, -1).astype(F32)

    w_dkv = mla_w_dkv[0]
    w_in = jnp.concatenate(
        [mla_w_dq[0], w_dkv[:, :MLA_KV_LORA], w_dkv[:, MLA_KV_LORA:],
         jnp.zeros((D, LANES - MLA_ROPE), F32)], axis=1).astype(BF16)
    w_uq = mla_w_uq[0].reshape(MLA_Q_LORA, H, MLA_NOPE + MLA_ROPE)
    w_uq_rope = jnp.pad(w_uq[:, :, MLA_NOPE:], ((0, 0), (0, 0), (0, LANES - MLA_ROPE)))
    w_uq2 = jnp.concatenate(
        [w_uq[:, :, :MLA_NOPE].reshape(MLA_Q_LORA, H * MLA_NOPE),
         w_uq_rope.reshape(MLA_Q_LORA, H * LANES)], axis=1).astype(BF16)
    w_ukv = mla_w_ukv[0].reshape(MLA_KV_LORA, H, MLA_NOPE + MLA_V)
    w_uk = w_ukv[:, :, :MLA_NOPE].reshape(MLA_KV_LORA, H * MLA_NOPE).astype(BF16)
    w_uvt = w_ukv[:, :, MLA_NOPE:].reshape(MLA_KV_LORA, H * MLA_V).T.astype(BF16)

    mla_scale = (MLA_NOPE + MLA_ROPE) ** -0.5 * LOG2E
    q, k, vt = _mla_proj(x, cos_mla, sin_mla, w_in, row(mla_q_norm[0]), row(mla_kv_norm[0]),
                         w_uq2, w_uk, w_uvt, mla_scale)
    o = _mla_attn(q, k, vt)

    x2d = x.reshape(B * S, D)
    h = _post(x2d, o.reshape(B * S, -1), mla_w_o[0].astype(BF16), row(ln1_g[0]), row(ln1_b[0]),
              ffn_w_gate_up[0].astype(BF16), ffn_w_down[0].astype(BF16), row(ln2_g[0]), row(ln2_b[0]))

    qk_width = DIFF_HEADS * 2 * DIFF_HEAD_DIM
    diff_scale = DIFF_HEAD_DIM ** -0.5 * LOG2E
    lambda_init = 0.8 - 0.6 * math.exp(-0.3 * 1)
    dq, dk, dvt = _diff_proj(h.reshape(B, S, D), cos_diff, sin_diff, diff_w_q[0].astype(BF16),
                             kv_w[:, :qk_width].astype(BF16), kv_w[:, qk_width:].T.astype(BF16), diff_scale)
    do = _diff_attn(dq, dk, dvt, row(diff_lq1[0]), row(diff_lk1[0]), row(diff_lq2[0]), row(diff_lk2[0]),
                    row(diff_subln[0]), lambda_init)
    out = _post(h, do.reshape(B * S, -1), diff_w_o[0].astype(BF16), row(ln1_g[1]), row(ln1_b[1]),
                ffn_w_gate_up[1].astype(BF16), ffn_w_down[1].astype(BF16), row(ln2_g[1]), row(ln2_b[1]))
    return out.reshape(B, S, D)
```

```python
import functools
import math

import jax
import jax.numpy as jnp
from jax import lax
from jax.experimental import pallas as pl
from jax.experimental.pallas import tpu as pltpu

F32 = jnp.float32
BF16 = jnp.bfloat16

MLA_HEADS = 8
MLA_NOPE = 128
MLA_ROPE = 64
MLA_V = 128
MLA_Q_LORA = 384
MLA_KV_LORA = 256
DIFF_HEADS = 8
DIFF_HEAD_DIM = 64
ROPE_THETA = 10000.0
DEPTH = 2
DEEPNORM_ALPHA = (2.0 * DEPTH) ** 0.25
LN_EPS = 1e-5
RMS_EPS = 1e-6
LOG2E = math.log2(math.e)
NEG_BIG = -1e30

LANES = 128
BF16_SUBLANES = 16
TOKEN_TILE = 512
ATTN_TILE = 512
MLA_HEADS_PER_STEP = 4
DIFF_HEADS_PER_STEP = 4
FF_CHUNK = 256
VMEM_LIMIT = 56 * 1024 * 1024
VT_ROWS = LANES + BF16_SUBLANES


def _nt_dot(a, b):
    return lax.dot_general(a, b, (((1,), (1,)), ((), ())), preferred_element_type=F32)


def _dot(a, b):
    return jnp.dot(a, b, preferred_element_type=F32)


def _rope_chunk(x, cos, sin_signed):
    lane = lax.broadcasted_iota(jnp.int32, x.shape, 1)
    first_half = (lane % 64) < 32
    rot = jnp.where(first_half, pltpu.roll(x, LANES - 32, 1), pltpu.roll(x, 32, 1))
    return x * cos + rot * sin_signed


def _rope_rows(x, cos_t, sin_signed_t):
    rot = jnp.concatenate([x[32:64], x[0:32], x[96:128], x[64:96]], axis=0)
    return x * cos_t + rot * sin_signed_t


def _rms(x, g):
    return x * lax.rsqrt(jnp.mean(jnp.square(x), axis=-1, keepdims=True) + RMS_EPS) * g


def _layernorm(x, g, b):
    mu = jnp.mean(x, axis=-1, keepdims=True)
    xc = x - mu
    var = jnp.mean(jnp.square(xc), axis=-1, keepdims=True)
    return xc * lax.rsqrt(var + LN_EPS) * g + b


def _const_spec(shape, single_buffer=False):
    nd = len(shape)
    kw = {"pipeline_mode": pl.Buffered(1)} if single_buffer else {}
    return pl.BlockSpec(shape, lambda *_: (0,) * nd, **kw)


def _store_vt(vt_ref, h, vt):
    ts = vt.shape[1]
    vt_ref[0, h, 0, 0:LANES, :] = vt.astype(BF16)
    row = lax.broadcasted_iota(jnp.int32, (BF16_SUBLANES, ts), 0)
    vt_ref[0, h, 0, LANES:VT_ROWS, :] = jnp.where(row == 0, 1.0, 0.0).astype(BF16)


def _mla_proj_kernel(x_ref, cos_ref, sin_ref, cos_t_ref, sin_t_ref, w_in_ref, qn_ref, kvn_ref, w_uqt_ref,
                     w_uk_ref, w_uvt_ref, qt_ref, k_ref, vt_ref, *, q_scale):
    xb = x_ref[0].astype(BF16)
    y = _dot(xb, w_in_ref[...])
    cq = _rms(y[:, :MLA_Q_LORA], qn_ref[...])
    c = _rms(y[:, MLA_Q_LORA:MLA_Q_LORA + MLA_KV_LORA], kvn_ref[...])
    k_rope = _rope_chunk(y[:, MLA_Q_LORA + MLA_KV_LORA:], cos_ref[...], sin_ref[...]).astype(BF16)
    cos_t = cos_t_ref[...]
    sin_t = sin_t_ref[...]
    qt = _nt_dot(w_uqt_ref[...], cq.astype(BF16))
    cb = c.astype(BF16)
    k_nope = _dot(cb, w_uk_ref[...])
    vt = _nt_dot(w_uvt_ref[...], cb)
    rope_base = MLA_HEADS * MLA_NOPE
    for h in range(MLA_HEADS):
        lo, hi = h * LANES, (h + 1) * LANES
        qt_ref[0, h, 0:LANES, :] = (qt[lo:hi] * q_scale).astype(BF16)
        qt_ref[0, h, LANES:2 * LANES, :] = (
            _rope_rows(qt[rope_base + lo:rope_base + hi], cos_t, sin_t) * q_scale).astype(BF16)
        k_ref[0, h, :, 0:LANES] = k_nope[:, lo:hi].astype(BF16)
        k_ref[0, h, :, LANES:2 * LANES] = k_rope
        _store_vt(vt_ref, h, vt[lo:hi, :])


def _mla_proj(x, cos, sin, cos_t, sin_t, w_in, qn, kvn, w_uqt, w_uk, w_uvt, q_scale):
    B, S, D = x.shape
    ts = TOKEN_TILE
    H = MLA_HEADS
    out_shape = (
        jax.ShapeDtypeStruct((B, H, 2 * LANES, S), BF16),
        jax.ShapeDtypeStruct((B, H, S, 2 * LANES), BF16),
        jax.ShapeDtypeStruct((B, H, S // ts, VT_ROWS, ts), BF16),
    )
    return pl.pallas_call(
        functools.partial(_mla_proj_kernel, q_scale=q_scale),
        grid=(B, S // ts),
        in_specs=[
            pl.BlockSpec((1, ts, D), lambda b, s: (b, s, 0)),
            pl.BlockSpec((ts, LANES), lambda b, s: (s, 0)),
            pl.BlockSpec((ts, LANES), lambda b, s: (s, 0)),
            pl.BlockSpec((LANES, ts), lambda b, s: (0, s)),
            pl.BlockSpec((LANES, ts), lambda b, s: (0, s)),
            _const_spec(w_in.shape), _const_spec(qn.shape), _const_spec(kvn.shape),
            _const_spec(w_uqt.shape), _const_spec(w_uk.shape), _const_spec(w_uvt.shape),
        ],
        out_specs=(
            pl.BlockSpec((1, H, 2 * LANES, ts), lambda b, s: (b, 0, 0, s)),
            pl.BlockSpec((1, H, ts, 2 * LANES), lambda b, s: (b, 0, s, 0)),
            pl.BlockSpec((1, H, 1, VT_ROWS, ts), lambda b, s: (b, 0, s, 0, 0)),
        ),
        out_shape=out_shape,
        compiler_params=pltpu.CompilerParams(
            dimension_semantics=("parallel", "parallel"), vmem_limit_bytes=VMEM_LIMIT),
        name="mla_proj",
    )(x, cos, sin, cos_t, sin_t, w_in, qn, kvn, w_uqt, w_uk, w_uvt)


def _flash_loop(streams, qi, s_ref, m_ref, acc_ref):
    @pl.when(qi == 0)
    def _seed():
        for i, (q_cur, _, k_block, _) in enumerate(streams):
            s_ref[i] = _dot(k_block(0), q_cur())

    m_ref[...] = jnp.full(m_ref.shape, NEG_BIG, F32)
    acc_ref[...] = jnp.zeros(acc_ref.shape, F32)

    def step(j, j_produce, use_next_tile, diagonal):
        for i, (q_cur, q_next, k_block, vt_block) in enumerate(streams):
            s = s_ref[i]
            if diagonal:
                row = lax.broadcasted_iota(jnp.int32, s.shape, 0)
                col = lax.broadcasted_iota(jnp.int32, s.shape, 1)
                s = jnp.where(row <= col, s, NEG_BIG)
            m_prev = m_ref[i]
            m_new = jnp.maximum(m_prev, jnp.max(s, axis=0, keepdims=True))
            alpha = jnp.exp2(m_prev - m_new)
            p = jnp.exp2(s - m_new).astype(BF16)
            m_ref[i] = m_new
            s_ref[i] = _dot(k_block(j_produce), q_next() if use_next_tile else q_cur())
            acc_ref[i] = acc_ref[i] * alpha + _dot(vt_block(j), p)

    def body(j, carry):
        step(j, j + 1, False, False)
        return carry

    lax.fori_loop(0, qi, body, 0)
    step(qi, 0, True, True)


def _normalised(acc_ref, i):
    return acc_ref[i, 0:LANES, :] * (1.0 / acc_ref[i, LANES:LANES + 1, :])


def _mla_attn_kernel(q_ref, qn_ref, k_ref, vt_ref, o_ref, s_ref, m_ref, acc_ref):
    hps, tq = q_ref.shape[1], q_ref.shape[3]
    qi = pl.program_id(2)
    streams = [(lambda h=h: q_ref[0, h],
                lambda h=h: qn_ref[0, h],
                lambda j, h=h: k_ref[0, h, pl.ds(pl.multiple_of(j * tq, tq), tq), :],
                lambda j, h=h: vt_ref[0, h, j]) for h in range(hps)]
    _flash_loop(streams, qi, s_ref, m_ref, acc_ref)
    for h in range(hps):
        o_ref[0, :, h * LANES:(h + 1) * LANES] = _normalised(acc_ref, h).T.astype(BF16)


def _mla_attn(qt, k, vt):
    B, H, Dk, S = qt.shape
    tq = ATTN_TILE
    hps = MLA_HEADS_PER_STEP
    nkv = vt.shape[2]
    nq = S // tq
    return pl.pallas_call(
        _mla_attn_kernel,
        grid=(B, H // hps, nq),
        in_specs=[
            pl.BlockSpec((1, hps, Dk, tq), lambda b, h, i: (b, h, 0, i)),
            pl.BlockSpec((1, hps, Dk, tq), lambda b, h, i: (b, h, 0, jnp.minimum(i + 1, nq - 1))),
            pl.BlockSpec((1, hps, S, Dk), lambda b, h, i: (b, h, 0, 0)),
            pl.BlockSpec((1, hps, nkv, VT_ROWS, tq), lambda b, h, i: (b, h, 0, 0, 0)),
        ],
        out_specs=pl.BlockSpec((1, tq, hps * LANES), lambda b, h, i: (b, i, h)),
        out_shape=jax.ShapeDtypeStruct((B, S, H * LANES), BF16),
        scratch_shapes=[pltpu.VMEM((hps, tq, tq), F32), pltpu.VMEM((hps, 1, tq), F32),
                        pltpu.VMEM((hps, VT_ROWS, tq), F32)],
        compiler_params=pltpu.CompilerParams(
            dimension_semantics=("parallel", "parallel", "arbitrary"), vmem_limit_bytes=VMEM_LIMIT),
        name="mla_attn",
    )(qt, qt, k, vt)


def _diff_proj_kernel(h_ref, cos_ref, sin_ref, cos_t_ref, sin_t_ref, wk_ref, wqvt_ref, qt_ref, k_ref, vt_ref,
                      *, q_scale):
    hb = h_ref[0].astype(BF16)
    k = _dot(hb, wk_ref[...])
    qvt = _nt_dot(wqvt_ref[...], hb)
    cos = cos_ref[...]
    sin = sin_ref[...]
    cos_t = cos_t_ref[...]
    sin_t = sin_t_ref[...]
    v_base = DIFF_HEADS * LANES
    for h in range(DIFF_HEADS):
        lo, hi = h * LANES, (h + 1) * LANES
        qt_ref[0, lo:hi, :] = (_rope_rows(qvt[lo:hi], cos_t, sin_t) * q_scale).astype(BF16)
        k_ref[0, :, lo:hi] = _rope_chunk(k[:, lo:hi], cos, sin).astype(BF16)
        _store_vt(vt_ref, h, qvt[v_base + lo:v_base + hi])


def _diff_proj(h, cos, sin, cos_t, sin_t, wk, wqvt, q_scale):
    B, S, D = h.shape
    ts = TOKEN_TILE
    H = DIFF_HEADS
    W = H * LANES
    out_shape = (
        jax.ShapeDtypeStruct((B, W, S), BF16),
        jax.ShapeDtypeStruct((B, S, W), BF16),
        jax.ShapeDtypeStruct((B, H, S // ts, VT_ROWS, ts), BF16),
    )
    return pl.pallas_call(
        functools.partial(_diff_proj_kernel, q_scale=q_scale),
        grid=(B, S // ts),
        in_specs=[
            pl.BlockSpec((1, ts, D), lambda b, s: (b, s, 0)),
            pl.BlockSpec((ts, LANES), lambda b, s: (s, 0)),
            pl.BlockSpec((ts, LANES), lambda b, s: (s, 0)),
            pl.BlockSpec((LANES, ts), lambda b, s: (0, s)),
            pl.BlockSpec((LANES, ts), lambda b, s: (0, s)),
            _const_spec(wk.shape), _const_spec(wqvt.shape),
        ],
        out_specs=(
            pl.BlockSpec((1, W, ts), lambda b, s: (b, 0, s)),
            pl.BlockSpec((1, ts, W), lambda b, s: (b, s, 0)),
            pl.BlockSpec((1, H, 1, VT_ROWS, ts), lambda b, s: (b, 0, s, 0, 0)),
        ),
        out_shape=out_shape,
        compiler_params=pltpu.CompilerParams(
            dimension_semantics=("parallel", "parallel"), vmem_limit_bytes=VMEM_LIMIT),
        name="diff_proj",
    )(h, cos, sin, cos_t, sin_t, wk, wqvt)


def _diff_attn_kernel(q_ref, qn_ref, k_ref, vt_ref, lq1_ref, lk1_ref, lq2_ref, lk2_ref, subln_ref, o_ref,
                      qmap_ref, s_ref, m_ref, acc_ref, *, lambda_init):
    tq = q_ref.shape[2]
    hps = vt_ref.shape[1]
    qi = pl.program_id(2)
    feature = lax.broadcasted_iota(jnp.int32, (LANES, tq), 0)

    def one_map(ref, h, c):
        q = ref[0, h * LANES:(h + 1) * LANES, :]
        keep = (feature < DIFF_HEAD_DIM) if c == 0 else (feature >= DIFF_HEAD_DIM)
        return jnp.where(keep, q, jnp.zeros_like(q))

    for i in range(2 * hps):
        qmap_ref[i] = one_map(q_ref, i // 2, i % 2)
    streams = [(lambda i=i: qmap_ref[i],
                lambda i=i: one_map(qn_ref, i // 2, i % 2),
                lambda j, h=i // 2: k_ref[0, pl.ds(pl.multiple_of(j * tq, tq), tq), h * LANES:(h + 1) * LANES],
                lambda j, h=i // 2: vt_ref[0, h, j]) for i in range(2 * hps)]
    _flash_loop(streams, qi, s_ref, m_ref, acc_ref)
    lam = (jnp.exp(jnp.sum(lq1_ref[...] * lk1_ref[...], axis=-1, keepdims=True))
           - jnp.exp(jnp.sum(lq2_ref[...] * lk2_ref[...], axis=-1, keepdims=True)) + lambda_init)
    for h in range(hps):
        o = (_normalised(acc_ref, 2 * h) - lam * _normalised(acc_ref, 2 * h + 1)).T
        o = _rms(o, subln_ref[...]) * (1.0 - lambda_init)
        o_ref[0, :, h * LANES:(h + 1) * LANES] = o.astype(BF16)


def _diff_attn(qt, k, vt, lq1, lk1, lq2, lk2, subln, lambda_init):
    B, W, S = qt.shape
    H = DIFF_HEADS
    tq = ATTN_TILE
    hps = DIFF_HEADS_PER_STEP
    nkv = vt.shape[2]
    nq = S // tq
    small = lambda a: _const_spec(a.shape)
    return pl.pallas_call(
        functools.partial(_diff_attn_kernel, lambda_init=lambda_init),
        grid=(B, H // hps, nq),
        in_specs=[
            pl.BlockSpec((1, hps * LANES, tq), lambda b, h, i: (b, h, i)),
            pl.BlockSpec((1, hps * LANES, tq), lambda b, h, i: (b, h, jnp.minimum(i + 1, nq - 1))),
            pl.BlockSpec((1, S, hps * LANES), lambda b, h, i: (b, 0, h)),
            pl.BlockSpec((1, hps, nkv, VT_ROWS, tq), lambda b, h, i: (b, h, 0, 0, 0)),
            small(lq1), small(lk1), small(lq2), small(lk2), small(subln),
        ],
        out_specs=pl.BlockSpec((1, tq, hps * LANES), lambda b, h, i: (b, i, h)),
        out_shape=jax.ShapeDtypeStruct((B, S, H * LANES), BF16),
        scratch_shapes=[
            pltpu.VMEM((2 * hps, LANES, tq), BF16),
            pltpu.VMEM((2 * hps, tq, tq), F32),
            pltpu.VMEM((2 * hps, 1, tq), F32),
            pltpu.VMEM((2 * hps, VT_ROWS, tq), F32),
        ],
        compiler_params=pltpu.CompilerParams(
            dimension_semantics=("parallel", "parallel", "arbitrary"), vmem_limit_bytes=VMEM_LIMIT),
        name="diff_attn",
    )(qt, qt, k, vt, lq1, lk1, lq2, lk2, subln)


def _post_kernel(x_ref, o_ref, wo_ref, g1_ref, b1_ref, wgu_ref, wd_ref, g2_ref, b2_ref, out_ref, *, d_ff):
    a = _dot(o_ref[...], wo_ref[...])
    h1 = _layernorm(DEEPNORM_ALPHA * x_ref[...] + a, g1_ref[...], b1_ref[...])
    hb = h1.astype(BF16)
    acc = jnp.zeros(h1.shape, F32)
    for lo in range(0, d_ff, FF_CHUNK):
        hi = min(lo + FF_CHUNK, d_ff)
        g = _dot(hb, wgu_ref[:, lo:hi])
        u = _dot(hb, wgu_ref[:, d_ff + lo:d_ff + hi])
        act = (g * jax.nn.sigmoid(g) * u).astype(BF16)
        acc = acc + _dot(act, wd_ref[lo:hi, :])
    out_ref[...] = _layernorm(DEEPNORM_ALPHA * h1 + acc, g2_ref[...], b2_ref[...])


def _post(x2d, o2d, wo, g1, b1, wgu, wd, g2, b2):
    T, D = x2d.shape
    ts = TOKEN_TILE
    d_ff = wd.shape[0]
    return pl.pallas_call(
        functools.partial(_post_kernel, d_ff=d_ff),
        grid=(T // ts,),
        in_specs=[
            pl.BlockSpec((ts, D), lambda t: (t, 0)),
            pl.BlockSpec((ts, o2d.shape[1]), lambda t: (t, 0)),
            _const_spec(wo.shape, True), _const_spec(g1.shape), _const_spec(b1.shape),
            _const_spec(wgu.shape, True), _const_spec(wd.shape, True),
            _const_spec(g2.shape), _const_spec(b2.shape),
        ],
        out_specs=pl.BlockSpec((ts, D), lambda t: (t, 0)),
        out_shape=jax.ShapeDtypeStruct((T, D), F32),
        compiler_params=pltpu.CompilerParams(
            dimension_semantics=("parallel",), vmem_limit_bytes=VMEM_LIMIT),
        name="post",
    )(x2d, o2d, wo, g1, b1, wgu, wd, g2, b2)


def _rope_tables(seq_len, dim):
    inv = 1.0 / (ROPE_THETA ** (jnp.arange(0, dim, 2, dtype=F32) / dim))
    ang = jnp.arange(seq_len, dtype=F32)[:, None] * inv[None, :]
    cos = jnp.cos(ang)
    sin = jnp.sin(ang)
    return jnp.concatenate([cos, cos], axis=-1), jnp.concatenate([-sin, sin], axis=-1)


def kernel(x, mla_w_dq, mla_q_norm, mla_w_uq, mla_w_dkv, mla_kv_norm, mla_w_ukv, mla_w_o, kv_w, diff_w_q, diff_lq1, diff_lk1, diff_lq2, diff_lk2, diff_subln, diff_w_o, ln1_g, ln1_b, ln2_g, ln2_b, ffn_w_gate_up, ffn_w_down):
    B, S, D = x.shape
    H = MLA_HEADS
    assert S % ATTN_TILE == 0 and S % TOKEN_TILE == 0 and TOKEN_TILE == ATTN_TILE
    assert MLA_ROPE == DIFF_HEAD_DIM == 64 and MLA_NOPE == MLA_V == LANES and 2 * DIFF_HEAD_DIM == LANES
    assert MLA_HEADS % MLA_HEADS_PER_STEP == 0 and DIFF_HEADS % DIFF_HEADS_PER_STEP == 0

    cos64, sin64 = _rope_tables(S, MLA_ROPE)
    zeros64 = jnp.zeros_like(cos64)
    cos_mla = jnp.concatenate([cos64, zeros64], axis=-1)
    sin_mla = jnp.concatenate([sin64, zeros64], axis=-1)
    cos_diff = jnp.concatenate([cos64, cos64], axis=-1)
    sin_diff = jnp.concatenate([sin64, sin64], axis=-1)
    cos_mla_t, sin_mla_t, cos_diff_t, sin_diff_t = cos_mla.T, sin_mla.T, cos_diff.T, sin_diff.T

    row = lambda v: v.reshape(1, -1).astype(F32)

    w_dkv = mla_w_dkv[0]
    w_in = jnp.concatenate(
        [mla_w_dq[0], w_dkv[:, :MLA_KV_LORA], w_dkv[:, MLA_KV_LORA:],
         jnp.zeros((D, LANES - MLA_ROPE), F32)], axis=1).astype(BF16)
    w_uq = mla_w_uq[0].reshape(MLA_Q_LORA, H, MLA_NOPE + MLA_ROPE)
    w_uq_rope = jnp.pad(w_uq[:, :, MLA_NOPE:], ((0, 0), (0, 0), (0, LANES - MLA_ROPE)))
    w_uqt = jnp.concatenate(
        [w_uq[:, :, :MLA_NOPE].reshape(MLA_Q_LORA, H * MLA_NOPE),
         w_uq_rope.reshape(MLA_Q_LORA, H * LANES)], axis=1).T.astype(BF16)
    w_ukv = mla_w_ukv[0].reshape(MLA_KV_LORA, H, MLA_NOPE + MLA_V)
    w_uk = w_ukv[:, :, :MLA_NOPE].reshape(MLA_KV_LORA, H * MLA_NOPE).astype(BF16)
    w_uvt = w_ukv[:, :, MLA_NOPE:].reshape(MLA_KV_LORA, H * MLA_V).T.astype(BF16)

    mla_scale = (MLA_NOPE + MLA_ROPE) ** -0.5 * LOG2E
    qt, k, vt = _mla_proj(x, cos_mla, sin_mla, cos_mla_t, sin_mla_t, w_in, row(mla_q_norm[0]),
                          row(mla_kv_norm[0]), w_uqt, w_uk, w_uvt, mla_scale)
    o = _mla_attn(qt, k, vt)

    x2d = x.reshape(B * S, D)
    h = _post(x2d, o.reshape(B * S, -1), mla_w_o[0].astype(BF16), row(ln1_g[0]), row(ln1_b[0]),
              ffn_w_gate_up[0].astype(BF16), ffn_w_down[0].astype(BF16), row(ln2_g[0]), row(ln2_b[0]))

    qk_width = DIFF_HEADS * 2 * DIFF_HEAD_DIM
    diff_scale = DIFF_HEAD_DIM ** -0.5 * LOG2E
    lambda_init = 0.8 - 0.6 * math.exp(-0.3 * 1)
    w_qvt = jnp.concatenate([diff_w_q[0], kv_w[:, qk_width:]], axis=1).T.astype(BF16)
    dqt, dk, dvt = _diff_proj(h.reshape(B, S, D), cos_diff, sin_diff, cos_diff_t, sin_diff_t,
                              kv_w[:, :qk_width].astype(BF16), w_qvt, diff_scale)
    do = _diff_attn(dqt, dk, dvt, row(diff_lq1[0]), row(diff_lk1[0]), row(diff_lq2[0]), row(diff_lk2[0]),
                    row(diff_subln[0]), lambda_init)
    out = _post(h, do.reshape(B * S, -1), diff_w_o[0].astype(BF16), row(ln1_g[1]), row(ln1_b[1]),
                ffn_w_gate_up[1].astype(BF16), ffn_w_down[1].astype(BF16), row(ln2_g[1]), row(ln2_b[1]))
    return out.reshape(B, S, D)
```

```python
import functools
import math

import jax
import jax.numpy as jnp
from jax import lax
from jax.experimental import pallas as pl
from jax.experimental.pallas import tpu as pltpu

F32 = jnp.float32
BF16 = jnp.bfloat16

MLA_HEADS = 8
MLA_NOPE = 128
MLA_ROPE = 64
MLA_V = 128
MLA_Q_LORA = 384
MLA_KV_LORA = 256
DIFF_HEADS = 8
DIFF_HEAD_DIM = 64
ROPE_THETA = 10000.0
DEPTH = 2
DEEPNORM_ALPHA = (2.0 * DEPTH) ** 0.25
LN_EPS = 1e-5
RMS_EPS = 1e-6
LOG2E = math.log2(math.e)
NEG_BIG = -1e30

LANES = 128
BF16_SUBLANES = 16
TOKEN_TILE = 512
ATTN_TILE = 512
MLA_HEADS_PER_STEP = 4
DIFF_HEADS_PER_STEP = 4
FF_CHUNK = 256
LN_PIECE_ROWS = 64
VMEM_LIMIT = 56 * 1024 * 1024
VT_ROWS = LANES + BF16_SUBLANES


def _nt_dot(a, b):
    return lax.dot_general(a, b, (((1,), (1,)), ((), ())), preferred_element_type=F32)


def _dot(a, b):
    return jnp.dot(a, b, preferred_element_type=F32)


def _rope_chunk(x, cos, sin_signed):
    lane = lax.broadcasted_iota(jnp.int32, x.shape, 1)
    first_half = (lane % 64) < 32
    rot = jnp.where(first_half, pltpu.roll(x, LANES - 32, 1), pltpu.roll(x, 32, 1))
    return x * cos + rot * sin_signed


def _rope_rows(x, cos_t, sin_signed_t):
    rot = jnp.concatenate([x[32:64], x[0:32], x[96:128], x[64:96]], axis=0)
    return x * cos_t + rot * sin_signed_t


def _rms(x, g):
    return x * lax.rsqrt(jnp.mean(jnp.square(x), axis=-1, keepdims=True) + RMS_EPS) * g


def _layernorm(x, g, b):
    mu = jnp.mean(x, axis=-1, keepdims=True)
    xc = x - mu
    var = jnp.mean(jnp.square(xc), axis=-1, keepdims=True)
    return xc * lax.rsqrt(var + LN_EPS) * g + b


def _const_spec(shape, single_buffer=False):
    nd = len(shape)
    kw = {"pipeline_mode": pl.Buffered(1)} if single_buffer else {}
    return pl.BlockSpec(shape, lambda *_: (0,) * nd, **kw)


def _store_vt(vt_ref, h, vt):
    ts = vt.shape[1]
    vt_ref[0, h, 0, 0:LANES, :] = vt.astype(BF16)
    row = lax.broadcasted_iota(jnp.int32, (BF16_SUBLANES, ts), 0)
    vt_ref[0, h, 0, LANES:VT_ROWS, :] = jnp.where(row == 0, 1.0, 0.0).astype(BF16)


def _mla_proj_kernel(x_ref, cos_ref, sin_ref, cos_t_ref, sin_t_ref, w_in_ref, qn_ref, kvn_ref, w_uqt_ref,
                     w_uk_ref, w_uvt_ref, qt_ref, k_ref, vt_ref, *, q_scale):
    xb = x_ref[0].astype(BF16)
    y = _dot(xb, w_in_ref[...])
    cq = _rms(y[:, :MLA_Q_LORA], qn_ref[...])
    c = _rms(y[:, MLA_Q_LORA:MLA_Q_LORA + MLA_KV_LORA], kvn_ref[...])
    k_rope = _rope_chunk(y[:, MLA_Q_LORA + MLA_KV_LORA:], cos_ref[...], sin_ref[...]).astype(BF16)
    cos_t = cos_t_ref[...]
    sin_t = sin_t_ref[...]
    qt = _nt_dot(w_uqt_ref[...], cq.astype(BF16))
    cb = c.astype(BF16)
    k_nope = _dot(cb, w_uk_ref[...])
    vt = _nt_dot(w_uvt_ref[...], cb)
    rope_base = MLA_HEADS * MLA_NOPE
    for h in range(MLA_HEADS):
        lo, hi = h * LANES, (h + 1) * LANES
        qt_ref[0, h, 0:LANES, :] = (qt[lo:hi] * q_scale).astype(BF16)
        qt_ref[0, h, LANES:2 * LANES, :] = (
            _rope_rows(qt[rope_base + lo:rope_base + hi], cos_t, sin_t) * q_scale).astype(BF16)
        k_ref[0, h, :, 0:LANES] = k_nope[:, lo:hi].astype(BF16)
        k_ref[0, h, :, LANES:2 * LANES] = k_rope
        _store_vt(vt_ref, h, vt[lo:hi, :])


def _mla_proj(x, cos, sin, cos_t, sin_t, w_in, qn, kvn, w_uqt, w_uk, w_uvt, q_scale):
    B, S, D = x.shape
    ts = TOKEN_TILE
    H = MLA_HEADS
    out_shape = (
        jax.ShapeDtypeStruct((B, H, 2 * LANES, S), BF16),
        jax.ShapeDtypeStruct((B, H, S, 2 * LANES), BF16),
        jax.ShapeDtypeStruct((B, H, S // ts, VT_ROWS, ts), BF16),
    )
    return pl.pallas_call(
        functools.partial(_mla_proj_kernel, q_scale=q_scale),
        grid=(B, S // ts),
        in_specs=[
            pl.BlockSpec((1, ts, D), lambda b, s: (b, s, 0)),
            pl.BlockSpec((ts, LANES), lambda b, s: (s, 0)),
            pl.BlockSpec((ts, LANES), lambda b, s: (s, 0)),
            pl.BlockSpec((LANES, ts), lambda b, s: (0, s)),
            pl.BlockSpec((LANES, ts), lambda b, s: (0, s)),
            _const_spec(w_in.shape), _const_spec(qn.shape), _const_spec(kvn.shape),
            _const_spec(w_uqt.shape), _const_spec(w_uk.shape), _const_spec(w_uvt.shape),
        ],
        out_specs=(
            pl.BlockSpec((1, H, 2 * LANES, ts), lambda b, s: (b, 0, 0, s)),
            pl.BlockSpec((1, H, ts, 2 * LANES), lambda b, s: (b, 0, s, 0)),
            pl.BlockSpec((1, H, 1, VT_ROWS, ts), lambda b, s: (b, 0, s, 0, 0)),
        ),
        out_shape=out_shape,
        compiler_params=pltpu.CompilerParams(
            dimension_semantics=("parallel", "parallel"), vmem_limit_bytes=VMEM_LIMIT),
        name="mla_proj",
    )(x, cos, sin, cos_t, sin_t, w_in, qn, kvn, w_uqt, w_uk, w_uvt)


def _flash_loop(streams, qi, s_ref, mx_ref, m_ref, acc_ref):
    @pl.when(qi == 0)
    def _seed():
        for i, (q_cur, _, k_block, _) in enumerate(streams):
            s_ref[i] = _dot(k_block(0), q_cur())

    m_ref[...] = jnp.full(m_ref.shape, NEG_BIG, F32)
    acc_ref[...] = jnp.zeros(acc_ref.shape, F32)
    tq = s_ref.shape[2]
    half = tq // 2

    def produce(i, k_blk, q):
        s_new = _dot(k_blk, q)
        s_ref[i] = s_new
        mx_ref[i] = jnp.max(s_new, axis=0, keepdims=True)

    def full_step(j):
        for i, (q_cur, _, k_block, vt_block) in enumerate(streams):
            m_prev = m_ref[i]
            m_new = jnp.maximum(m_prev, mx_ref[i])
            alpha = jnp.exp2(m_prev - m_new)
            p = jnp.exp2(s_ref[i] - m_new).astype(BF16)
            m_ref[i] = m_new
            produce(i, k_block(j + 1), q_cur())
            acc_ref[i] = acc_ref[i] * alpha + _dot(vt_block(j), p)

    def diagonal_step(j):
        row = lax.broadcasted_iota(jnp.int32, (half, half), 0)
        col = lax.broadcasted_iota(jnp.int32, (half, half), 1)
        visible = row <= col
        for i, (_, q_next, k_block, vt_block) in enumerate(streams):
            s_tl = jnp.where(visible, s_ref[i, 0:half, 0:half], NEG_BIG)
            s_tr = s_ref[i, 0:half, half:tq]
            s_br = jnp.where(visible, s_ref[i, half:tq, half:tq], NEG_BIG)
            m_cur = jnp.concatenate(
                [jnp.max(s_tl, axis=0, keepdims=True),
                 jnp.maximum(jnp.max(s_tr, axis=0, keepdims=True), jnp.max(s_br, axis=0, keepdims=True))], axis=1)
            m_prev = m_ref[i]
            m_new = jnp.maximum(m_prev, m_cur)
            alpha = jnp.exp2(m_prev - m_new)
            m_ref[i] = m_new
            p_tl = jnp.exp2(s_tl - m_new[:, 0:half]).astype(BF16)
            p_tr = jnp.exp2(s_tr - m_new[:, half:tq]).astype(BF16)
            p_br = jnp.exp2(s_br - m_new[:, half:tq]).astype(BF16)
            produce(i, k_block(0), q_next())
            vt = vt_block(j)
            acc_ref[i, :, 0:half] = acc_ref[i, :, 0:half] * alpha[:, 0:half] + _dot(vt[:, 0:half], p_tl)
            acc_ref[i, :, half:tq] = (acc_ref[i, :, half:tq] * alpha[:, half:tq]
                                      + _dot(vt[:, 0:half], p_tr) + _dot(vt[:, half:tq], p_br))

    def body(j, carry):
        full_step(j)
        return carry

    lax.fori_loop(0, qi, body, 0)
    diagonal_step(qi)


def _normalised(acc_ref, i):
    return acc_ref[i, 0:LANES, :] * (1.0 / acc_ref[i, LANES:LANES + 1, :])


def _mla_attn_kernel(q_ref, qn_ref, k_ref, vt_ref, o_ref, s_ref, mx_ref, m_ref, acc_ref):
    hps, tq = q_ref.shape[1], q_ref.shape[3]
    qi = pl.program_id(2)
    streams = [(lambda h=h: q_ref[0, h],
                lambda h=h: qn_ref[0, h],
                lambda j, h=h: k_ref[0, h, pl.ds(pl.multiple_of(j * tq, tq), tq), :],
                lambda j, h=h: vt_ref[0, h, j]) for h in range(hps)]
    _flash_loop(streams, qi, s_ref, mx_ref, m_ref, acc_ref)
    for h in range(hps):
        o_ref[0, :, h * LANES:(h + 1) * LANES] = _normalised(acc_ref, h).T.astype(BF16)


def _mla_attn(qt, k, vt):
    B, H, Dk, S = qt.shape
    tq = ATTN_TILE
    hps = MLA_HEADS_PER_STEP
    nkv = vt.shape[2]
    nq = S // tq
    return pl.pallas_call(
        _mla_attn_kernel,
        grid=(B, H // hps, nq),
        in_specs=[
            pl.BlockSpec((1, hps, Dk, tq), lambda b, h, i: (b, h, 0, i)),
            pl.BlockSpec((1, hps, Dk, tq), lambda b, h, i: (b, h, 0, jnp.minimum(i + 1, nq - 1))),
            pl.BlockSpec((1, hps, S, Dk), lambda b, h, i: (b, h, 0, 0)),
            pl.BlockSpec((1, hps, nkv, VT_ROWS, tq), lambda b, h, i: (b, h, 0, 0, 0)),
        ],
        out_specs=pl.BlockSpec((1, tq, hps * LANES), lambda b, h, i: (b, i, h)),
        out_shape=jax.ShapeDtypeStruct((B, S, H * LANES), BF16),
        scratch_shapes=[pltpu.VMEM((hps, tq, tq), F32), pltpu.VMEM((hps, 1, tq), F32), pltpu.VMEM((hps, 1, tq), F32),
                        pltpu.VMEM((hps, VT_ROWS, tq), F32)],
        compiler_params=pltpu.CompilerParams(
            dimension_semantics=("parallel", "parallel", "arbitrary"), vmem_limit_bytes=VMEM_LIMIT),
        name="mla_attn",
    )(qt, qt, k, vt)


def _diff_proj_kernel(h_ref, cos_ref, sin_ref, cos_t_ref, sin_t_ref, wk_ref, wqvt_ref, qt_ref, k_ref, vt_ref,
                      *, q_scale):
    hb = h_ref[0].astype(BF16)
    v_base = DIFF_HEADS * LANES
    qt = _nt_dot(wqvt_ref[0:v_base, :], hb)
    k = _dot(hb, wk_ref[...])
    vt = _nt_dot(wqvt_ref[v_base:2 * v_base, :], hb)
    cos = cos_ref[...]
    sin = sin_ref[...]
    cos_t = cos_t_ref[...]
    sin_t = sin_t_ref[...]
    for h in range(DIFF_HEADS):
        lo, hi = h * LANES, (h + 1) * LANES
        qt_ref[0, lo:hi, :] = (_rope_rows(qt[lo:hi], cos_t, sin_t) * q_scale).astype(BF16)
    for h in range(DIFF_HEADS):
        lo, hi = h * LANES, (h + 1) * LANES
        k_ref[0, :, lo:hi] = _rope_chunk(k[:, lo:hi], cos, sin).astype(BF16)
    for h in range(DIFF_HEADS):
        lo, hi = h * LANES, (h + 1) * LANES
        _store_vt(vt_ref, h, vt[lo:hi])


def _diff_proj(h, cos, sin, cos_t, sin_t, wk, wqvt, q_scale):
    B, S, D = h.shape
    ts = TOKEN_TILE
    H = DIFF_HEADS
    W = H * LANES
    out_shape = (
        jax.ShapeDtypeStruct((B, W, S), BF16),
        jax.ShapeDtypeStruct((B, S, W), BF16),
        jax.ShapeDtypeStruct((B, H, S // ts, VT_ROWS, ts), BF16),
    )
    return pl.pallas_call(
        functools.partial(_diff_proj_kernel, q_scale=q_scale),
        grid=(B, S // ts),
        in_specs=[
            pl.BlockSpec((1, ts, D), lambda b, s: (b, s, 0)),
            pl.BlockSpec((ts, LANES), lambda b, s: (s, 0)),
            pl.BlockSpec((ts, LANES), lambda b, s: (s, 0)),
            pl.BlockSpec((LANES, ts), lambda b, s: (0, s)),
            pl.BlockSpec((LANES, ts), lambda b, s: (0, s)),
            _const_spec(wk.shape), _const_spec(wqvt.shape),
        ],
        out_specs=(
            pl.BlockSpec((1, W, ts), lambda b, s: (b, 0, s)),
            pl.BlockSpec((1, ts, W), lambda b, s: (b, s, 0)),
            pl.BlockSpec((1, H, 1, VT_ROWS, ts), lambda b, s: (b, 0, s, 0, 0)),
        ),
        out_shape=out_shape,
        compiler_params=pltpu.CompilerParams(
            dimension_semantics=("parallel", "parallel"), vmem_limit_bytes=VMEM_LIMIT),
        name="diff_proj",
    )(h, cos, sin, cos_t, sin_t, wk, wqvt)


def _diff_attn_kernel(q_ref, qn_ref, k_ref, vt_ref, lq1_ref, lk1_ref, lq2_ref, lk2_ref, subln_ref, o_ref,
                      qmap_ref, s_ref, mx_ref, m_ref, acc_ref, *, lambda_init):
    tq = q_ref.shape[2]
    hps = vt_ref.shape[1]
    qi = pl.program_id(2)
    feature = lax.broadcasted_iota(jnp.int32, (LANES, tq), 0)

    def one_map(ref, h, c):
        q = ref[0, h * LANES:(h + 1) * LANES, :]
        keep = (feature < DIFF_HEAD_DIM) if c == 0 else (feature >= DIFF_HEAD_DIM)
        return jnp.where(keep, q, jnp.zeros_like(q))

    for i in range(2 * hps):
        qmap_ref[i] = one_map(q_ref, i // 2, i % 2)
    streams = [(lambda i=i: qmap_ref[i],
                lambda i=i: one_map(qn_ref, i // 2, i % 2),
                lambda j, h=i // 2: k_ref[0, pl.ds(pl.multiple_of(j * tq, tq), tq), h * LANES:(h + 1) * LANES],
                lambda j, h=i // 2: vt_ref[0, h, j]) for i in range(2 * hps)]
    _flash_loop(streams, qi, s_ref, mx_ref, m_ref, acc_ref)
    lam = (jnp.exp(jnp.sum(lq1_ref[...] * lk1_ref[...], axis=-1, keepdims=True))
           - jnp.exp(jnp.sum(lq2_ref[...] * lk2_ref[...], axis=-1, keepdims=True)) + lambda_init)
    for h in range(hps):
        o = (_normalised(acc_ref, 2 * h) - lam * _normalised(acc_ref, 2 * h + 1)).T
        o = _rms(o, subln_ref[...]) * (1.0 - lambda_init)
        o_ref[0, :, h * LANES:(h + 1) * LANES] = o.astype(BF16)


def _diff_attn(qt, k, vt, lq1, lk1, lq2, lk2, subln, lambda_init):
    B, W, S = qt.shape
    H = DIFF_HEADS
    tq = ATTN_TILE
    hps = DIFF_HEADS_PER_STEP
    nkv = vt.shape[2]
    nq = S // tq
    small = lambda a: _const_spec(a.shape)
    return pl.pallas_call(
        functools.partial(_diff_attn_kernel, lambda_init=lambda_init),
        grid=(B, H // hps, nq),
        in_specs=[
            pl.BlockSpec((1, hps * LANES, tq), lambda b, h, i: (b, h, i)),
            pl.BlockSpec((1, hps * LANES, tq), lambda b, h, i: (b, h, jnp.minimum(i + 1, nq - 1))),
            pl.BlockSpec((1, S, hps * LANES), lambda b, h, i: (b, 0, h)),
            pl.BlockSpec((1, hps, nkv, VT_ROWS, tq), lambda b, h, i: (b, h, 0, 0, 0)),
            small(lq1), small(lk1), small(lq2), small(lk2), small(subln),
        ],
        out_specs=pl.BlockSpec((1, tq, hps * LANES), lambda b, h, i: (b, i, h)),
        out_shape=jax.ShapeDtypeStruct((B, S, H * LANES), BF16),
        scratch_shapes=[
            pltpu.VMEM((2 * hps, LANES, tq), BF16),
            pltpu.VMEM((2 * hps, tq, tq), F32),
            pltpu.VMEM((2 * hps, 1, tq), F32),
            pltpu.VMEM((2 * hps, 1, tq), F32),
            pltpu.VMEM((2 * hps, VT_ROWS, tq), F32),
        ],
        compiler_params=pltpu.CompilerParams(
            dimension_semantics=("parallel", "parallel", "arbitrary"), vmem_limit_bytes=VMEM_LIMIT),
        name="diff_attn",
    )(qt, qt, k, vt, lq1, lk1, lq2, lk2, subln)


def _zero_after(v):
    bits = pltpu.bitcast(v[0:8, 0:LANES], jnp.int32)
    zero = lax.shift_right_logical(lax.shift_right_logical(bits, 16), 16)
    return zero[0:1, 0:1].astype(F32)


def _post_kernel(x0_ref, o0_ref, xn_ref, on_ref, wo_ref, g1_ref, b1_ref, wgu_ref, wd_ref, g2_ref, b2_ref, out_ref,
                 h1_ref, resid_prev_ref, *, d_ff):
    ts = h1_ref.shape[0]

    @pl.when(pl.program_id(0) == 0)
    def _seed():
        a0 = _dot(o0_ref[...], wo_ref[...])
        h1_ref[...] = _layernorm(DEEPNORM_ALPHA * x0_ref[...] + a0, g1_ref[...], b1_ref[...])
        resid_prev_ref[...] = jnp.zeros(resid_prev_ref.shape, F32)

    h1 = h1_ref[...]
    hb = h1.astype(BF16)
    attn_next = _dot(on_ref[...], wo_ref[...])

    def ln2_piece(r, zero):
        out_ref[r, :] = _layernorm(resid_prev_ref[r, :] + zero, g2_ref[...], b2_ref[...])

    def ln1_piece(r, zero):
        h1_ref[r, :] = _layernorm(DEEPNORM_ALPHA * xn_ref[r, :] + attn_next[r, :] + zero, g1_ref[...], b1_ref[...])

    pieces = [(fn, slice(r0, r0 + LN_PIECE_ROWS))
              for r0 in range(0, ts, LN_PIECE_ROWS) for fn in (ln2_piece, ln1_piece)]
    chunk_starts = list(range(0, d_ff, FF_CHUNK))
    anchor_chunks = len(chunk_starts) - 1
    acc = jnp.zeros(h1.shape, F32)
    done = 0
    for c, lo in enumerate(chunk_starts):
        hi = min(lo + FF_CHUNK, d_ff)
        g = _dot(hb, wgu_ref[:, lo:hi])
        u = _dot(hb, wgu_ref[:, d_ff + lo:d_ff + hi])
        act = (g * jax.nn.sigmoid(g) * u).astype(BF16)
        acc = acc + _dot(act, wd_ref[lo:hi, :])
        if c < anchor_chunks:
            target = (len(pieces) * (c + 1)) // anchor_chunks
            zero = _zero_after(acc)
            for fn, r in pieces[done:target]:
                fn(r, zero)
            done = target
    resid_prev_ref[...] = DEEPNORM_ALPHA * h1 + acc


def _post(x2d, o2d, wo, g1, b1, wgu, wd, g2, b2):
    T, D = x2d.shape
    ts = TOKEN_TILE
    nt = T // ts
    d_ff = wd.shape[0]
    do = o2d.shape[1]
    first = lambda t: (0, 0)
    following = lambda t: (jnp.minimum(t + 1, nt - 1), 0)
    return pl.pallas_call(
        functools.partial(_post_kernel, d_ff=d_ff),
        grid=(nt + 1,),
        in_specs=[
            pl.BlockSpec((ts, D), first, pipeline_mode=pl.Buffered(1)),
            pl.BlockSpec((ts, do), first, pipeline_mode=pl.Buffered(1)),
            pl.BlockSpec((ts, D), following),
            pl.BlockSpec((ts, do), following),
            _const_spec(wo.shape, True), _const_spec(g1.shape), _const_spec(b1.shape),
            _const_spec(wgu.shape, True), _const_spec(wd.shape, True),
            _const_spec(g2.shape), _const_spec(b2.shape),
        ],
        out_specs=pl.BlockSpec((ts, D), lambda t: (jnp.maximum(t - 1, 0), 0)),
        out_shape=jax.ShapeDtypeStruct((T, D), F32),
        scratch_shapes=[pltpu.VMEM((ts, D), F32), pltpu.VMEM((ts, D), F32)],
        compiler_params=pltpu.CompilerParams(
            dimension_semantics=("arbitrary",), vmem_limit_bytes=VMEM_LIMIT),
        name="post",
    )(x2d, o2d, x2d, o2d, wo, g1, b1, wgu, wd, g2, b2)


def _rope_tables(seq_len, dim):
    inv = 1.0 / (ROPE_THETA ** (jnp.arange(0, dim, 2, dtype=F32) / dim))
    ang = jnp.arange(seq_len, dtype=F32)[:, None] * inv[None, :]
    cos = jnp.cos(ang)
    sin = jnp.sin(ang)
    return jnp.concatenate([cos, cos], axis=-1), jnp.concatenate([-sin, sin], axis=-1)


def kernel(x, mla_w_dq, mla_q_norm, mla_w_uq, mla_w_dkv, mla_kv_norm, mla_w_ukv, mla_w_o, kv_w, diff_w_q, diff_lq1, diff_lk1, diff_lq2, diff_lk2, diff_subln, diff_w_o, ln1_g, ln1_b, ln2_g, ln2_b, ffn_w_gate_up, ffn_w_down):
    B, S, D = x.shape
    H = MLA_HEADS
    assert S % ATTN_TILE == 0 and S % TOKEN_TILE == 0 and TOKEN_TILE == ATTN_TILE
    assert MLA_ROPE == DIFF_HEAD_DIM == 64 and MLA_NOPE == MLA_V == LANES and 2 * DIFF_HEAD_DIM == LANES
    assert MLA_HEADS % MLA_HEADS_PER_STEP == 0 and DIFF_HEADS % DIFF_HEADS_PER_STEP == 0

    cos64, sin64 = _rope_tables(S, MLA_ROPE)
    zeros64 = jnp.zeros_like(cos64)
    cos_mla = jnp.concatenate([cos64, zeros64], axis=-1)
    sin_mla = jnp.concatenate([sin64, zeros64], axis=-1)
    cos_diff = jnp.concatenate([cos64, cos64], axis=-1)
    sin_diff = jnp.concatenate([sin64, sin64], axis=-1)
    cos_mla_t, sin_mla_t, cos_diff_t, sin_diff_t = cos_mla.T, sin_mla.T, cos_diff.T, sin_diff.T

    row = lambda v: v.reshape(1, -1).astype(F32)

    w_dkv = mla_w_dkv[0]
    w_in = jnp.concatenate(
        [mla_w_dq[0], w_dkv[:, :MLA_KV_LORA], w_dkv[:, MLA_KV_LORA:],
         jnp.zeros((D, LANES - MLA_ROPE), F32)], axis=1).astype(BF16)
    w_uq = mla_w_uq[0].reshape(MLA_Q_LORA, H, MLA_NOPE + MLA_ROPE)
    w_uq_rope = jnp.pad(w_uq[:, :, MLA_NOPE:], ((0, 0), (0, 0), (0, LANES - MLA_ROPE)))
    w_uqt = jnp.concatenate(
        [w_uq[:, :, :MLA_NOPE].reshape(MLA_Q_LORA, H * MLA_NOPE),
         w_uq_rope.reshape(MLA_Q_LORA, H * LANES)], axis=1).T.astype(BF16)
    w_ukv = mla_w_ukv[0].reshape(MLA_KV_LORA, H, MLA_NOPE + MLA_V)
    w_uk = w_ukv[:, :, :MLA_NOPE].reshape(MLA_KV_LORA, H * MLA_NOPE).astype(BF16)
    w_uvt = w_ukv[:, :, MLA_NOPE:].reshape(MLA_KV_LORA, H * MLA_V).T.astype(BF16)

    mla_scale = (MLA_NOPE + MLA_ROPE) ** -0.5 * LOG2E
    qt, k, vt = _mla_proj(x, cos_mla, sin_mla, cos_mla_t, sin_mla_t, w_in, row(mla_q_norm[0]),
                          row(mla_kv_norm[0]), w_uqt, w_uk, w_uvt, mla_scale)
    o = _mla_attn(qt, k, vt)

    x2d = x.reshape(B * S, D)
    h = _post(x2d, o.reshape(B * S, -1), mla_w_o[0].astype(BF16), row(ln1_g[0]), row(ln1_b[0]),
              ffn_w_gate_up[0].astype(BF16), ffn_w_down[0].astype(BF16), row(ln2_g[0]), row(ln2_b[0]))

    qk_width = DIFF_HEADS * 2 * DIFF_HEAD_DIM
    diff_scale = DIFF_HEAD_DIM ** -0.5 * LOG2E
    lambda_init = 0.8 - 0.6 * math.exp(-0.3 * 1)
    w_qvt = jnp.concatenate([diff_w_q[0], kv_w[:, qk_width:]], axis=1).T.astype(BF16)
    dqt, dk, dvt = _diff_proj(h.reshape(B, S, D), cos_diff, sin_diff, cos_diff_t, sin_diff_t,
                              kv_w[:, :qk_width].astype(BF16), w_qvt, diff_scale)
    do = _diff_attn(dqt, dk, dvt, row(diff_lq1[0]), row(diff_lk1[0]), row(diff_lq2[0]), row(diff_lk2[0]),
                    row(diff_subln[0]), lambda_init)
    out = _post(h, do.reshape(B * S, -1), diff_w_o[0].astype(BF16), row(ln1_g[1]), row(ln1_b[1]),
                ffn_w_gate_up[1].astype(BF16), ffn_w_down[1].astype(BF16), row(ln2_g[1]), row(ln2_b[1]))
    return out.reshape(B, S, D)
```

```python
import functools
import math

import jax
import jax.numpy as jnp
from jax import lax
from jax.experimental import pallas as pl
from jax.experimental.pallas import tpu as pltpu

F32 = jnp.float32
BF16 = jnp.bfloat16

MLA_HEADS = 8
MLA_NOPE = 128
MLA_ROPE = 64
MLA_V = 128
MLA_Q_LORA = 384
MLA_KV_LORA = 256
DIFF_HEADS = 8
DIFF_HEAD_DIM = 64
ROPE_THETA = 10000.0
DEPTH = 2
DEEPNORM_ALPHA = (2.0 * DEPTH) ** 0.25
LN_EPS = 1e-5
RMS_EPS = 1e-6
LOG2E = math.log2(math.e)
NEG_BIG = -1e30

LANES = 128
BF16_SUBLANES = 16
TOKEN_TILE = 512
ATTN_TILE = 512
MLA_HEADS_PER_STEP = 4
DIFF_HEADS_PER_STEP = 4
FF_CHUNK = 256
LN_PIECE_ROWS = 64
VMEM_LIMIT = 56 * 1024 * 1024
VT_ROWS = LANES + BF16_SUBLANES


def _nt_dot(a, b):
    return lax.dot_general(a, b, (((1,), (1,)), ((), ())), preferred_element_type=F32)


def _dot(a, b):
    return jnp.dot(a, b, preferred_element_type=F32)


def _rope_chunk(x, cos, sin_signed):
    lane = lax.broadcasted_iota(jnp.int32, x.shape, 1)
    first_half = (lane % 64) < 32
    rot = jnp.where(first_half, pltpu.roll(x, LANES - 32, 1), pltpu.roll(x, 32, 1))
    return x * cos + rot * sin_signed


def _rope_rows(x, cos_t, sin_signed_t):
    rot = jnp.concatenate([x[32:64], x[0:32], x[96:128], x[64:96]], axis=0)
    return x * cos_t + rot * sin_signed_t


def _rms(x, g):
    return x * lax.rsqrt(jnp.mean(jnp.square(x), axis=-1, keepdims=True) + RMS_EPS) * g


def _layernorm(x, g, b):
    mu = jnp.mean(x, axis=-1, keepdims=True)
    xc = x - mu
    var = jnp.mean(jnp.square(xc), axis=-1, keepdims=True)
    return xc * lax.rsqrt(var + LN_EPS) * g + b


def _const_spec(shape, single_buffer=False):
    nd = len(shape)
    kw = {"pipeline_mode": pl.Buffered(1)} if single_buffer else {}
    return pl.BlockSpec(shape, lambda *_: (0,) * nd, **kw)


def _store_vt(vt_ref, h, vt):
    ts = vt.shape[1]
    vt_ref[0, h, 0, 0:LANES, :] = vt.astype(BF16)
    row = lax.broadcasted_iota(jnp.int32, (BF16_SUBLANES, ts), 0)
    vt_ref[0, h, 0, LANES:VT_ROWS, :] = jnp.where(row == 0, 1.0, 0.0).astype(BF16)


def _mla_proj_kernel(x_ref, cos_ref, sin_ref, cos_t_ref, sin_t_ref, w_in_ref, qn_ref, kvn_ref, w_uqt_ref,
                     w_uk_ref, w_uvt_ref, qt_ref, kn_ref, kr_ref, vt_ref, *, q_scale):
    xb = x_ref[0].astype(BF16)
    y = _dot(xb, w_in_ref[...])
    cq = _rms(y[:, :MLA_Q_LORA], qn_ref[...])
    c = _rms(y[:, MLA_Q_LORA:MLA_Q_LORA + MLA_KV_LORA], kvn_ref[...])
    kr_ref[0] = _rope_chunk(y[:, MLA_Q_LORA + MLA_KV_LORA:], cos_ref[...], sin_ref[...]).astype(BF16)
    cos_t = cos_t_ref[...]
    sin_t = sin_t_ref[...]
    qt = _nt_dot(w_uqt_ref[...], cq.astype(BF16))
    cb = c.astype(BF16)
    k_nope = _dot(cb, w_uk_ref[...])
    vt = _nt_dot(w_uvt_ref[...], cb)
    rope_base = MLA_HEADS * MLA_NOPE
    for h in range(MLA_HEADS):
        lo, hi = h * LANES, (h + 1) * LANES
        qt_ref[0, h, 0:LANES, :] = (qt[lo:hi] * q_scale).astype(BF16)
        qt_ref[0, h, LANES:2 * LANES, :] = (
            _rope_rows(qt[rope_base + lo:rope_base + hi], cos_t, sin_t) * q_scale).astype(BF16)
        kn_ref[0, h] = k_nope[:, lo:hi].astype(BF16)
        _store_vt(vt_ref, h, vt[lo:hi, :])


def _mla_proj(x, cos, sin, cos_t, sin_t, w_in, qn, kvn, w_uqt, w_uk, w_uvt, q_scale):
    B, S, D = x.shape
    ts = TOKEN_TILE
    H = MLA_HEADS
    out_shape = (
        jax.ShapeDtypeStruct((B, H, 2 * LANES, S), BF16),
        jax.ShapeDtypeStruct((B, H, S, LANES), BF16),
        jax.ShapeDtypeStruct((B, S, LANES), BF16),
        jax.ShapeDtypeStruct((B, H, S // ts, VT_ROWS, ts), BF16),
    )
    return pl.pallas_call(
        functools.partial(_mla_proj_kernel, q_scale=q_scale),
        grid=(B, S // ts),
        in_specs=[
            pl.BlockSpec((1, ts, D), lambda b, s: (b, s, 0)),
            pl.BlockSpec((ts, LANES), lambda b, s: (s, 0)),
            pl.BlockSpec((ts, LANES), lambda b, s: (s, 0)),
            pl.BlockSpec((LANES, ts), lambda b, s: (0, s)),
            pl.BlockSpec((LANES, ts), lambda b, s: (0, s)),
            _const_spec(w_in.shape), _const_spec(qn.shape), _const_spec(kvn.shape),
            _const_spec(w_uqt.shape), _const_spec(w_uk.shape), _const_spec(w_uvt.shape),
        ],
        out_specs=(
            pl.BlockSpec((1, H, 2 * LANES, ts), lambda b, s: (b, 0, 0, s)),
            pl.BlockSpec((1, H, ts, LANES), lambda b, s: (b, 0, s, 0)),
            pl.BlockSpec((1, ts, LANES), lambda b, s: (b, s, 0)),
            pl.BlockSpec((1, H, 1, VT_ROWS, ts), lambda b, s: (b, 0, s, 0, 0)),
        ),
        out_shape=out_shape,
        compiler_params=pltpu.CompilerParams(
            dimension_semantics=("parallel", "parallel"), vmem_limit_bytes=VMEM_LIMIT),
        name="mla_proj",
    )(x, cos, sin, cos_t, sin_t, w_in, qn, kvn, w_uqt, w_uk, w_uvt)


def _flash_loop(streams, qi, s_ref, mx_ref, m_ref, acc_ref):
    @pl.when(qi == 0)
    def _seed():
        for i, (q_cur, _, k_block, _) in enumerate(streams):
            s_ref[i] = _dot(k_block(0), q_cur())

    m_ref[...] = jnp.full(m_ref.shape, NEG_BIG, F32)
    acc_ref[...] = jnp.zeros(acc_ref.shape, F32)
    tq = s_ref.shape[2]
    half = tq // 2

    def produce(i, k_blk, q):
        s_new = _dot(k_blk, q)
        s_ref[i] = s_new
        mx_ref[i] = jnp.max(s_new, axis=0, keepdims=True)

    def full_step(j):
        for i, (q_cur, _, k_block, vt_block) in enumerate(streams):
            m_prev = m_ref[i]
            m_new = jnp.maximum(m_prev, mx_ref[i])
            alpha = jnp.exp2(m_prev - m_new)
            p = jnp.exp2(s_ref[i] - m_new).astype(BF16)
            m_ref[i] = m_new
            produce(i, k_block(j + 1), q_cur())
            acc_ref[i] = acc_ref[i] * alpha + _dot(vt_block(j), p)

    def diagonal_step(j):
        row = lax.broadcasted_iota(jnp.int32, (half, half), 0)
        col = lax.broadcasted_iota(jnp.int32, (half, half), 1)
        visible = row <= col
        for i, (_, q_next, k_block, vt_block) in enumerate(streams):
            s_tl = jnp.where(visible, s_ref[i, 0:half, 0:half], NEG_BIG)
            s_tr = s_ref[i, 0:half, half:tq]
            s_br = jnp.where(visible, s_ref[i, half:tq, half:tq], NEG_BIG)
            m_cur = jnp.concatenate(
                [jnp.max(s_tl, axis=0, keepdims=True),
                 jnp.maximum(jnp.max(s_tr, axis=0, keepdims=True), jnp.max(s_br, axis=0, keepdims=True))], axis=1)
            m_prev = m_ref[i]
            m_new = jnp.maximum(m_prev, m_cur)
            alpha = jnp.exp2(m_prev - m_new)
            m_ref[i] = m_new
            p_tl = jnp.exp2(s_tl - m_new[:, 0:half]).astype(BF16)
            p_tr = jnp.exp2(s_tr - m_new[:, half:tq]).astype(BF16)
            p_br = jnp.exp2(s_br - m_new[:, half:tq]).astype(BF16)
            produce(i, k_block(0), q_next())
            vt = vt_block(j)
            acc_ref[i, :, 0:half] = acc_ref[i, :, 0:half] * alpha[:, 0:half] + _dot(vt[:, 0:half], p_tl)
            acc_ref[i, :, half:tq] = (acc_ref[i, :, half:tq] * alpha[:, half:tq]
                                      + _dot(vt[:, 0:half], p_tr) + _dot(vt[:, half:tq], p_br))

    def body(pair, carry):
        full_step(2 * pair)
        full_step(2 * pair + 1)
        return carry

    lax.fori_loop(0, qi // 2, body, 0)

    @pl.when(qi % 2 == 1)
    def _odd_block():
        full_step(qi - 1)

    diagonal_step(qi)


def _normalised(acc_ref, i):
    return acc_ref[i, 0:LANES, :] * (1.0 / acc_ref[i, LANES:LANES + 1, :])


def _mla_attn_kernel(q_ref, qn_ref, kn_ref, kr_ref, vt_ref, o_ref, s_ref, mx_ref, m_ref, acc_ref):
    hps, tq = q_ref.shape[1], q_ref.shape[3]
    qi = pl.program_id(2)

    def key_block(j, h):
        rows = pl.ds(pl.multiple_of(j * tq, tq), tq)
        return jnp.concatenate([kn_ref[0, h, rows, :], kr_ref[0, rows, :]], axis=1)

    streams = [(lambda h=h: q_ref[0, h],
                lambda h=h: qn_ref[0, h],
                lambda j, h=h: key_block(j, h),
                lambda j, h=h: vt_ref[0, h, j]) for h in range(hps)]
    _flash_loop(streams, qi, s_ref, mx_ref, m_ref, acc_ref)
    for h in range(hps):
        o_ref[0, :, h * LANES:(h + 1) * LANES] = _normalised(acc_ref, h).T.astype(BF16)


def _mla_attn(qt, kn, kr, vt):
    B, H, Dk, S = qt.shape
    tq = ATTN_TILE
    hps = MLA_HEADS_PER_STEP
    nkv = vt.shape[2]
    nq = S // tq
    return pl.pallas_call(
        _mla_attn_kernel,
        grid=(B, H // hps, nq),
        in_specs=[
            pl.BlockSpec((1, hps, Dk, tq), lambda b, h, i: (b, h, 0, i)),
            pl.BlockSpec((1, hps, Dk, tq), lambda b, h, i: (b, h, 0, jnp.minimum(i + 1, nq - 1))),
            pl.BlockSpec((1, hps, S, LANES), lambda b, h, i: (b, h, 0, 0)),
            pl.BlockSpec((1, S, LANES), lambda b, h, i: (b, 0, 0)),
            pl.BlockSpec((1, hps, nkv, VT_ROWS, tq), lambda b, h, i: (b, h, 0, 0, 0)),
        ],
        out_specs=pl.BlockSpec((1, tq, hps * LANES), lambda b, h, i: (b, i, h)),
        out_shape=jax.ShapeDtypeStruct((B, S, H * LANES), BF16),
        scratch_shapes=[pltpu.VMEM((hps, tq, tq), F32), pltpu.VMEM((hps, 1, tq), F32), pltpu.VMEM((hps, 1, tq), F32),
                        pltpu.VMEM((hps, VT_ROWS, tq), F32)],
        compiler_params=pltpu.CompilerParams(
            dimension_semantics=("parallel", "parallel", "arbitrary"), vmem_limit_bytes=VMEM_LIMIT),
        name="mla_attn",
    )(qt, qt, kn, kr, vt)


def _diff_proj_kernel(h_ref, cos_ref, sin_ref, cos_t_ref, sin_t_ref, wk_ref, wqvt_ref, qt_ref, k_ref, vt_ref,
                      *, q_scale):
    hb = h_ref[0].astype(BF16)
    v_base = DIFF_HEADS * LANES
    qt = _nt_dot(wqvt_ref[0:v_base, :], hb)
    k = _dot(hb, wk_ref[...])
    vt = _nt_dot(wqvt_ref[v_base:2 * v_base, :], hb)
    cos = cos_ref[...]
    sin = sin_ref[...]
    cos_t = cos_t_ref[...]
    sin_t = sin_t_ref[...]
    for h in range(DIFF_HEADS):
        lo, hi = h * LANES, (h + 1) * LANES
        qt_ref[0, lo:hi, :] = (_rope_rows(qt[lo:hi], cos_t, sin_t) * q_scale).astype(BF16)
    for h in range(DIFF_HEADS):
        lo, hi = h * LANES, (h + 1) * LANES
        k_ref[0, :, lo:hi] = _rope_chunk(k[:, lo:hi], cos, sin).astype(BF16)
    for h in range(DIFF_HEADS):
        lo, hi = h * LANES, (h + 1) * LANES
        _store_vt(vt_ref, h, vt[lo:hi])


def _diff_proj(h, cos, sin, cos_t, sin_t, wk, wqvt, q_scale):
    B, S, D = h.shape
    ts = TOKEN_TILE
    H = DIFF_HEADS
    W = H * LANES
    out_shape = (
        jax.ShapeDtypeStruct((B, W, S), BF16),
        jax.ShapeDtypeStruct((B, S, W), BF16),
        jax.ShapeDtypeStruct((B, H, S // ts, VT_ROWS, ts), BF16),
    )
    return pl.pallas_call(
        functools.partial(_diff_proj_kernel, q_scale=q_scale),
        grid=(B, S // ts),
        in_specs=[
            pl.BlockSpec((1, ts, D), lambda b, s: (b, s, 0)),
            pl.BlockSpec((ts, LANES), lambda b, s: (s, 0)),
            pl.BlockSpec((ts, LANES), lambda b, s: (s, 0)),
            pl.BlockSpec((LANES, ts), lambda b, s: (0, s)),
            pl.BlockSpec((LANES, ts), lambda b, s: (0, s)),
            _const_spec(wk.shape), _const_spec(wqvt.shape),
        ],
        out_specs=(
            pl.BlockSpec((1, W, ts), lambda b, s: (b, 0, s)),
            pl.BlockSpec((1, ts, W), lambda b, s: (b, s, 0)),
            pl.BlockSpec((1, H, 1, VT_ROWS, ts), lambda b, s: (b, 0, s, 0, 0)),
        ),
        out_shape=out_shape,
        compiler_params=pltpu.CompilerParams(
            dimension_semantics=("parallel", "parallel"), vmem_limit_bytes=VMEM_LIMIT),
        name="diff_proj",
    )(h, cos, sin, cos_t, sin_t, wk, wqvt)


def _diff_attn_kernel(q_ref, qn_ref, k_ref, vt_ref, lq1_ref, lk1_ref, lq2_ref, lk2_ref, subln_ref, o_ref,
                      qmap_ref, s_ref, mx_ref, m_ref, acc_ref, *, lambda_init):
    tq = q_ref.shape[2]
    hps = vt_ref.shape[1]
    qi = pl.program_id(2)
    feature = lax.broadcasted_iota(jnp.int32, (LANES, tq), 0)

    def one_map(ref, h, c):
        q = ref[0, h * LANES:(h + 1) * LANES, :]
        keep = (feature < DIFF_HEAD_DIM) if c == 0 else (feature >= DIFF_HEAD_DIM)
        return jnp.where(keep, q, jnp.zeros_like(q))

    for i in range(2 * hps):
        qmap_ref[i] = one_map(q_ref, i // 2, i % 2)
    streams = [(lambda i=i: qmap_ref[i],
                lambda i=i: one_map(qn_ref, i // 2, i % 2),
                lambda j, h=i // 2: k_ref[0, pl.ds(pl.multiple_of(j * tq, tq), tq), h * LANES:(h + 1) * LANES],
                lambda j, h=i // 2: vt_ref[0, h, j]) for i in range(2 * hps)]
    _flash_loop(streams, qi, s_ref, mx_ref, m_ref, acc_ref)
    lam = (jnp.exp(jnp.sum(lq1_ref[...] * lk1_ref[...], axis=-1, keepdims=True))
           - jnp.exp(jnp.sum(lq2_ref[...] * lk2_ref[...], axis=-1, keepdims=True)) + lambda_init)
    for h in range(hps):
        o = (_normalised(acc_ref, 2 * h) - lam * _normalised(acc_ref, 2 * h + 1)).T
        o = _rms(o, subln_ref[...]) * (1.0 - lambda_init)
        o_ref[0, :, h * LANES:(h + 1) * LANES] = o.astype(BF16)


def _diff_attn(qt, k, vt, lq1, lk1, lq2, lk2, subln, lambda_init):
    B, W, S = qt.shape
    H = DIFF_HEADS
    tq = ATTN_TILE
    hps = DIFF_HEADS_PER_STEP
    nkv = vt.shape[2]
    nq = S // tq
    small = lambda a: _const_spec(a.shape)
    return pl.pallas_call(
        functools.partial(_diff_attn_kernel, lambda_init=lambda_init),
        grid=(B, H // hps, nq),
        in_specs=[
            pl.BlockSpec((1, hps * LANES, tq), lambda b, h, i: (b, h, i)),
            pl.BlockSpec((1, hps * LANES, tq), lambda b, h, i: (b, h, jnp.minimum(i + 1, nq - 1))),
            pl.BlockSpec((1, S, hps * LANES), lambda b, h, i: (b, 0, h)),
            pl.BlockSpec((1, hps, nkv, VT_ROWS, tq), lambda b, h, i: (b, h, 0, 0, 0)),
            small(lq1), small(lk1), small(lq2), small(lk2), small(subln),
        ],
        out_specs=pl.BlockSpec((1, tq, hps * LANES), lambda b, h, i: (b, i, h)),
        out_shape=jax.ShapeDtypeStruct((B, S, H * LANES), BF16),
        scratch_shapes=[
            pltpu.VMEM((2 * hps, LANES, tq), BF16),
            pltpu.VMEM((2 * hps, tq, tq), F32),
            pltpu.VMEM((2 * hps, 1, tq), F32),
            pltpu.VMEM((2 * hps, 1, tq), F32),
            pltpu.VMEM((2 * hps, VT_ROWS, tq), F32),
        ],
        compiler_params=pltpu.CompilerParams(
            dimension_semantics=("parallel", "parallel", "arbitrary"), vmem_limit_bytes=VMEM_LIMIT),
        name="diff_attn",
    )(qt, qt, k, vt, lq1, lk1, lq2, lk2, subln)


def _zero_after(v):
    bits = pltpu.bitcast(v[0:8, 0:LANES], jnp.int32)
    zero = lax.shift_right_logical(lax.shift_right_logical(bits, 16), 16)
    return zero[0:1, 0:1].astype(F32)


def _post_kernel(x0_ref, o0_ref, xn_ref, on_ref, wo_ref, g1_ref, b1_ref, wgu_ref, wd_ref, g2_ref, b2_ref, out_ref,
                 h1_ref, resid_prev_ref, *, d_ff):
    ts = h1_ref.shape[0]

    @pl.when(pl.program_id(0) == 0)
    def _seed():
        a0 = _dot(o0_ref[...], wo_ref[...])
        h1_ref[...] = _layernorm(DEEPNORM_ALPHA * x0_ref[...] + a0, g1_ref[...], b1_ref[...])
        resid_prev_ref[...] = jnp.zeros(resid_prev_ref.shape, F32)

    h1 = h1_ref[...]
    hb = h1.astype(BF16)
    attn_next = _dot(on_ref[...], wo_ref[...])

    def ln2_piece(r, zero):
        out_ref[r, :] = _layernorm(resid_prev_ref[r, :] + zero, g2_ref[...], b2_ref[...])

    def ln1_piece(r, zero):
        h1_ref[r, :] = _layernorm(DEEPNORM_ALPHA * xn_ref[r, :] + attn_next[r, :] + zero, g1_ref[...], b1_ref[...])

    pieces = [(fn, slice(r0, r0 + LN_PIECE_ROWS))
              for r0 in range(0, ts, LN_PIECE_ROWS) for fn in (ln2_piece, ln1_piece)]
    chunk_starts = list(range(0, d_ff, FF_CHUNK))
    anchor_chunks = len(chunk_starts) - 1
    acc = jnp.zeros(h1.shape, F32)
    done = 0
    for c, lo in enumerate(chunk_starts):
        hi = min(lo + FF_CHUNK, d_ff)
        g = _dot(hb, wgu_ref[:, lo:hi])
        u = _dot(hb, wgu_ref[:, d_ff + lo:d_ff + hi])
        act = (g * jax.nn.sigmoid(g) * u).astype(BF16)
        acc = acc + _dot(act, wd_ref[lo:hi, :])
        if c < anchor_chunks:
            target = (len(pieces) * (c + 1)) // anchor_chunks
            zero = _zero_after(acc)
            for fn, r in pieces[done:target]:
                fn(r, zero)
            done = target
    resid_prev_ref[...] = DEEPNORM_ALPHA * h1 + acc


def _post(x2d, o2d, wo, g1, b1, wgu, wd, g2, b2):
    T, D = x2d.shape
    ts = TOKEN_TILE
    nt = T // ts
    d_ff = wd.shape[0]
    do = o2d.shape[1]
    first = lambda t: (0, 0)
    following = lambda t: (jnp.minimum(t + 1, nt - 1), 0)
    return pl.pallas_call(
        functools.partial(_post_kernel, d_ff=d_ff),
        grid=(nt + 1,),
        in_specs=[
            pl.BlockSpec((ts, D), first, pipeline_mode=pl.Buffered(1)),
            pl.BlockSpec((ts, do), first, pipeline_mode=pl.Buffered(1)),
            pl.BlockSpec((ts, D), following),
            pl.BlockSpec((ts, do), following),
            _const_spec(wo.shape, True), _const_spec(g1.shape), _const_spec(b1.shape),
            _const_spec(wgu.shape, True), _const_spec(wd.shape, True),
            _const_spec(g2.shape), _const_spec(b2.shape),
        ],
        out_specs=pl.BlockSpec((ts, D), lambda t: (jnp.maximum(t - 1, 0), 0)),
        out_shape=jax.ShapeDtypeStruct((T, D), F32),
        scratch_shapes=[pltpu.VMEM((ts, D), F32), pltpu.VMEM((ts, D), F32)],
        compiler_params=pltpu.CompilerParams(
            dimension_semantics=("arbitrary",), vmem_limit_bytes=VMEM_LIMIT),
        name="post",
    )(x2d, o2d, x2d, o2d, wo, g1, b1, wgu, wd, g2, b2)


def _rope_tables(seq_len, dim):
    inv = 1.0 / (ROPE_THETA ** (jnp.arange(0, dim, 2, dtype=F32) / dim))
    ang = jnp.arange(seq_len, dtype=F32)[:, None] * inv[None, :]
    cos = jnp.cos(ang)
    sin = jnp.sin(ang)
    return jnp.concatenate([cos, cos], axis=-1), jnp.concatenate([-sin, sin], axis=-1)


def kernel(x, mla_w_dq, mla_q_norm, mla_w_uq, mla_w_dkv, mla_kv_norm, mla_w_ukv, mla_w_o, kv_w, diff_w_q, diff_lq1, diff_lk1, diff_lq2, diff_lk2, diff_subln, diff_w_o, ln1_g, ln1_b, ln2_g, ln2_b, ffn_w_gate_up, ffn_w_down):
    B, S, D = x.shape
    H = MLA_HEADS
    assert S % ATTN_TILE == 0 and S % TOKEN_TILE == 0 and TOKEN_TILE == ATTN_TILE
    assert MLA_ROPE == DIFF_HEAD_DIM == 64 and MLA_NOPE == MLA_V == LANES and 2 * DIFF_HEAD_DIM == LANES
    assert MLA_HEADS % MLA_HEADS_PER_STEP == 0 and DIFF_HEADS % DIFF_HEADS_PER_STEP == 0

    cos64, sin64 = _rope_tables(S, MLA_ROPE)
    zeros64 = jnp.zeros_like(cos64)
    cos_mla = jnp.concatenate([cos64, zeros64], axis=-1)
    sin_mla = jnp.concatenate([sin64, zeros64], axis=-1)
    cos_diff = jnp.concatenate([cos64, cos64], axis=-1)
    sin_diff = jnp.concatenate([sin64, sin64], axis=-1)
    cos_mla_t, sin_mla_t, cos_diff_t, sin_diff_t = cos_mla.T, sin_mla.T, cos_diff.T, sin_diff.T

    row = lambda v: v.reshape(1, -1).astype(F32)

    w_dkv = mla_w_dkv[0]
    w_in = jnp.concatenate(
        [mla_w_dq[0], w_dkv[:, :MLA_KV_LORA], w_dkv[:, MLA_KV_LORA:],
         jnp.zeros((D, LANES - MLA_ROPE), F32)], axis=1).astype(BF16)
    w_uq = mla_w_uq[0].reshape(MLA_Q_LORA, H, MLA_NOPE + MLA_ROPE)
    w_uq_rope = jnp.pad(w_uq[:, :, MLA_NOPE:], ((0, 0), (0, 0), (0, LANES - MLA_ROPE)))
    w_uqt = jnp.concatenate(
        [w_uq[:, :, :MLA_NOPE].reshape(MLA_Q_LORA, H * MLA_NOPE),
         w_uq_rope.reshape(MLA_Q_LORA, H * LANES)], axis=1).T.astype(BF16)
    w_ukv = mla_w_ukv[0].reshape(MLA_KV_LORA, H, MLA_NOPE + MLA_V)
    w_uk = w_ukv[:, :, :MLA_NOPE].reshape(MLA_KV_LORA, H * MLA_NOPE).astype(BF16)
    w_uvt = w_ukv[:, :, MLA_NOPE:].reshape(MLA_KV_LORA, H * MLA_V).T.astype(BF16)

    mla_scale = (MLA_NOPE + MLA_ROPE) ** -0.5 * LOG2E
    qt, kn, kr, vt = _mla_proj(x, cos_mla, sin_mla, cos_mla_t, sin_mla_t, w_in, row(mla_q_norm[0]),
                          row(mla_kv_norm[0]), w_uqt, w_uk, w_uvt, mla_scale)
    o = _mla_attn(qt, kn, kr, vt)

    x2d = x.reshape(B * S, D)
    h = _post(x2d, o.reshape(B * S, -1), mla_w_o[0].astype(BF16), row(ln1_g[0]), row(ln1_b[0]),
              ffn_w_gate_up[0].astype(BF16), ffn_w_down[0].astype(BF16), row(ln2_g[0]), row(ln2_b[0]))

    qk_width = DIFF_HEADS * 2 * DIFF_HEAD_DIM
    diff_scale = DIFF_HEAD_DIM ** -0.5 * LOG2E
    lambda_init = 0.8 - 0.6 * math.exp(-0.3 * 1)
    w_qvt = jnp.concatenate([diff_w_q[0], kv_w[:, qk_width:]], axis=1).T.astype(BF16)
    dqt, dk, dvt = _diff_proj(h.reshape(B, S, D), cos_diff, sin_diff, cos_diff_t, sin_diff_t,
                              kv_w[:, :qk_width].astype(BF16), w_qvt, diff_scale)
    do = _diff_attn(dqt, dk, dvt, row(diff_lq1[0]), row(diff_lk1[0]), row(diff_lq2[0]), row(diff_lk2[0]),
                    row(diff_subln[0]), lambda_init)
    out = _post(h, do.reshape(B * S, -1), diff_w_o[0].astype(BF16), row(ln1_g[1]), row(ln1_b[1]),
                ffn_w_gate_up[1].astype(BF16), ffn_w_down[1].astype(BF16), row(ln2_g[1]), row(ln2_b[1]))
    return out.reshape(B, S, D)
```

```python
import functools
import math

import jax
import jax.numpy as jnp
from jax import lax
from jax.experimental import pallas as pl
from jax.experimental.pallas import tpu as pltpu

F32 = jnp.float32
BF16 = jnp.bfloat16

MLA_HEADS = 8
MLA_NOPE = 128
MLA_ROPE = 64
MLA_V = 128
MLA_Q_LORA = 384
MLA_KV_LORA = 256
DIFF_HEADS = 8
DIFF_HEAD_DIM = 64
ROPE_THETA = 10000.0
DEPTH = 2
DEEPNORM_ALPHA = (2.0 * DEPTH) ** 0.25
LN_EPS = 1e-5
RMS_EPS = 1e-6
LOG2E = math.log2(math.e)
NEG_BIG = -1e30

LANES = 128
BF16_SUBLANES = 16
TOKEN_TILE = 512
ATTN_TILE = 512
MLA_HEADS_PER_STEP = 4
DIFF_HEADS_PER_STEP = 4
FF_CHUNK = 256
LN_PIECE_ROWS = 64
VMEM_LIMIT = 56 * 1024 * 1024
VT_ROWS = LANES + BF16_SUBLANES


def _nt_dot(a, b):
    return lax.dot_general(a, b, (((1,), (1,)), ((), ())), preferred_element_type=F32)


def _dot(a, b):
    return jnp.dot(a, b, preferred_element_type=F32)


def _rope_chunk(x, cos, sin_signed):
    lane = lax.broadcasted_iota(jnp.int32, x.shape, 1)
    first_half = (lane % 64) < 32
    rot = jnp.where(first_half, pltpu.roll(x, LANES - 32, 1), pltpu.roll(x, 32, 1))
    return x * cos + rot * sin_signed


def _rope_rows(x, cos_t, sin_signed_t):
    rot = jnp.concatenate([x[32:64], x[0:32], x[96:128], x[64:96]], axis=0)
    return x * cos_t + rot * sin_signed_t


def _rms(x, g):
    return x * lax.rsqrt(jnp.mean(jnp.square(x), axis=-1, keepdims=True) + RMS_EPS) * g


def _layernorm(x, g, b):
    mu = jnp.mean(x, axis=-1, keepdims=True)
    xc = x - mu
    var = jnp.mean(jnp.square(xc), axis=-1, keepdims=True)
    return xc * lax.rsqrt(var + LN_EPS) * g + b


def _const_spec(shape, single_buffer=False):
    nd = len(shape)
    kw = {"pipeline_mode": pl.Buffered(1)} if single_buffer else {}
    return pl.BlockSpec(shape, lambda *_: (0,) * nd, **kw)


def _store_vt(vt_ref, h, vt):
    ts = vt.shape[1]
    vt_ref[0, h, 0, 0:LANES, :] = vt.astype(BF16)
    row = lax.broadcasted_iota(jnp.int32, (BF16_SUBLANES, ts), 0)
    vt_ref[0, h, 0, LANES:VT_ROWS, :] = jnp.where(row == 0, 1.0, 0.0).astype(BF16)


def _mla_proj_kernel(x_ref, cos_ref, sin_ref, cos_t_ref, sin_t_ref, w_in_ref, qn_ref, kvn_ref, w_uqt_ref,
                     w_uk_ref, w_uvt_ref, qt_ref, kn_ref, kr_ref, vt_ref, *, q_scale):
    xb = x_ref[0].astype(BF16)
    y = _dot(xb, w_in_ref[...])
    cq = _rms(y[:, :MLA_Q_LORA], qn_ref[...])
    c = _rms(y[:, MLA_Q_LORA:MLA_Q_LORA + MLA_KV_LORA], kvn_ref[...])
    kr_ref[0] = _rope_chunk(y[:, MLA_Q_LORA + MLA_KV_LORA:], cos_ref[...], sin_ref[...]).astype(BF16)
    cos_t = cos_t_ref[...]
    sin_t = sin_t_ref[...]
    qt = _nt_dot(w_uqt_ref[...], cq.astype(BF16))
    cb = c.astype(BF16)
    k_nope = _dot(cb, w_uk_ref[...])
    vt = _nt_dot(w_uvt_ref[...], cb)
    rope_base = MLA_HEADS * MLA_NOPE
    for h in range(MLA_HEADS):
        lo, hi = h * LANES, (h + 1) * LANES
        qt_ref[0, h, 0:LANES, :] = (qt[lo:hi] * q_scale).astype(BF16)
        qt_ref[0, h, LANES:2 * LANES, :] = (
            _rope_rows(qt[rope_base + lo:rope_base + hi], cos_t, sin_t) * q_scale).astype(BF16)
        kn_ref[0, h] = k_nope[:, lo:hi].astype(BF16)
        _store_vt(vt_ref, h, vt[lo:hi, :])


def _mla_proj(x, cos, sin, cos_t, sin_t, w_in, qn, kvn, w_uqt, w_uk, w_uvt, q_scale):
    B, S, D = x.shape
    ts = TOKEN_TILE
    H = MLA_HEADS
    out_shape = (
        jax.ShapeDtypeStruct((B, H, 2 * LANES, S), BF16),
        jax.ShapeDtypeStruct((B, H, S, LANES), BF16),
        jax.ShapeDtypeStruct((B, S, LANES), BF16),
        jax.ShapeDtypeStruct((B, H, S // ts, VT_ROWS, ts), BF16),
    )
    return pl.pallas_call(
        functools.partial(_mla_proj_kernel, q_scale=q_scale),
        grid=(B, S // ts),
        in_specs=[
            pl.BlockSpec((1, ts, D), lambda b, s: (b, s, 0)),
            pl.BlockSpec((ts, LANES), lambda b, s: (s, 0)),
            pl.BlockSpec((ts, LANES), lambda b, s: (s, 0)),
            pl.BlockSpec((LANES, ts), lambda b, s: (0, s)),
            pl.BlockSpec((LANES, ts), lambda b, s: (0, s)),
            _const_spec(w_in.shape), _const_spec(qn.shape), _const_spec(kvn.shape),
            _const_spec(w_uqt.shape), _const_spec(w_uk.shape), _const_spec(w_uvt.shape),
        ],
        out_specs=(
            pl.BlockSpec((1, H, 2 * LANES, ts), lambda b, s: (b, 0, 0, s)),
            pl.BlockSpec((1, H, ts, LANES), lambda b, s: (b, 0, s, 0)),
            pl.BlockSpec((1, ts, LANES), lambda b, s: (b, s, 0)),
            pl.BlockSpec((1, H, 1, VT_ROWS, ts), lambda b, s: (b, 0, s, 0, 0)),
        ),
        out_shape=out_shape,
        compiler_params=pltpu.CompilerParams(
            dimension_semantics=("parallel", "parallel"), vmem_limit_bytes=VMEM_LIMIT),
        name="mla_proj",
    )(x, cos, sin, cos_t, sin_t, w_in, qn, kvn, w_uqt, w_uk, w_uvt)


def _flash_loop(streams, qi, s_ref, mx_ref, m_ref, acc_ref):
    @pl.when(qi == 0)
    def _seed():
        for i, (q_cur, _, k_block, _) in enumerate(streams):
            s_ref[i] = _dot(k_block(0), q_cur())

    m_ref[...] = jnp.full(m_ref.shape, NEG_BIG, F32)
    acc_ref[...] = jnp.zeros(acc_ref.shape, F32)
    tq = s_ref.shape[2]
    half = tq // 2

    def produce(i, k_blk, q):
        s_new = _dot(k_blk, q)
        s_ref[i] = s_new
        mx_ref[i] = jnp.max(s_new, axis=0, keepdims=True)

    def full_step(j):
        for i, (q_cur, _, k_block, vt_block) in enumerate(streams):
            m_prev = m_ref[i]
            m_new = jnp.maximum(m_prev, mx_ref[i])
            alpha = jnp.exp2(m_prev - m_new)
            p = jnp.exp2(s_ref[i] - m_new).astype(BF16)
            m_ref[i] = m_new
            produce(i, k_block(j + 1), q_cur())
            acc_ref[i] = acc_ref[i] * alpha + _dot(vt_block(j), p)

    def diagonal_step(j):
        row = lax.broadcasted_iota(jnp.int32, (half, half), 0)
        col = lax.broadcasted_iota(jnp.int32, (half, half), 1)
        visible = row <= col
        for i, (_, q_next, k_block, vt_block) in enumerate(streams):
            s_tl = jnp.where(visible, s_ref[i, 0:half, 0:half], NEG_BIG)
            s_tr = s_ref[i, 0:half, half:tq]
            s_br = jnp.where(visible, s_ref[i, half:tq, half:tq], NEG_BIG)
            m_cur = jnp.concatenate(
                [jnp.max(s_tl, axis=0, keepdims=True),
                 jnp.maximum(jnp.max(s_tr, axis=0, keepdims=True), jnp.max(s_br, axis=0, keepdims=True))], axis=1)
            m_prev = m_ref[i]
            m_new = jnp.maximum(m_prev, m_cur)
            alpha = jnp.exp2(m_prev - m_new)
            m_ref[i] = m_new
            p_tl = jnp.exp2(s_tl - m_new[:, 0:half]).astype(BF16)
            p_tr = jnp.exp2(s_tr - m_new[:, half:tq]).astype(BF16)
            p_br = jnp.exp2(s_br - m_new[:, half:tq]).astype(BF16)
            produce(i, k_block(0), q_next())
            vt = vt_block(j)
            acc_ref[i, :, 0:half] = acc_ref[i, :, 0:half] * alpha[:, 0:half] + _dot(vt[:, 0:half], p_tl)
            acc_ref[i, :, half:tq] = (acc_ref[i, :, half:tq] * alpha[:, half:tq]
                                      + _dot(vt[:, 0:half], p_tr) + _dot(vt[:, half:tq], p_br))

    def body(pair, carry):
        full_step(2 * pair)
        full_step(2 * pair + 1)
        return carry

    lax.fori_loop(0, qi // 2, body, 0)

    @pl.when(qi % 2 == 1)
    def _odd_block():
        full_step(qi - 1)

    diagonal_step(qi)


def _normalised(acc_ref, i):
    return acc_ref[i, 0:LANES, :] * (1.0 / acc_ref[i, LANES:LANES + 1, :])


def _mla_attn_kernel(q_ref, qn_ref, kn_ref, kr_ref, vt_ref, o_ref, s_ref, mx_ref, m_ref, acc_ref):
    hps, tq = q_ref.shape[1], q_ref.shape[3]
    qi = pl.program_id(2)

    def key_block(j, h):
        rows = pl.ds(pl.multiple_of(j * tq, tq), tq)
        return jnp.concatenate([kn_ref[0, h, rows, :], kr_ref[0, rows, :]], axis=1)

    streams = [(lambda h=h: q_ref[0, h],
                lambda h=h: qn_ref[0, h],
                lambda j, h=h: key_block(j, h),
                lambda j, h=h: vt_ref[0, h, j]) for h in range(hps)]
    _flash_loop(streams, qi, s_ref, mx_ref, m_ref, acc_ref)
    for h in range(hps):
        o_ref[0, :, h * LANES:(h + 1) * LANES] = _normalised(acc_ref, h).T.astype(BF16)


def _mla_attn(qt, kn, kr, vt):
    B, H, Dk, S = qt.shape
    tq = ATTN_TILE
    hps = MLA_HEADS_PER_STEP
    nkv = vt.shape[2]
    nq = S // tq
    return pl.pallas_call(
        _mla_attn_kernel,
        grid=(B, H // hps, nq),
        in_specs=[
            pl.BlockSpec((1, hps, Dk, tq), lambda b, h, i: (b, h, 0, i)),
            pl.BlockSpec((1, hps, Dk, tq), lambda b, h, i: (b, h, 0, jnp.minimum(i + 1, nq - 1))),
            pl.BlockSpec((1, hps, S, LANES), lambda b, h, i: (b, h, 0, 0)),
            pl.BlockSpec((1, S, LANES), lambda b, h, i: (b, 0, 0)),
            pl.BlockSpec((1, hps, nkv, VT_ROWS, tq), lambda b, h, i: (b, h, 0, 0, 0)),
        ],
        out_specs=pl.BlockSpec((1, tq, hps * LANES), lambda b, h, i: (b, i, h)),
        out_shape=jax.ShapeDtypeStruct((B, S, H * LANES), BF16),
        scratch_shapes=[pltpu.VMEM((hps, tq, tq), F32), pltpu.VMEM((hps, 1, tq), F32), pltpu.VMEM((hps, 1, tq), F32),
                        pltpu.VMEM((hps, VT_ROWS, tq), F32)],
        compiler_params=pltpu.CompilerParams(
            dimension_semantics=("parallel", "parallel", "arbitrary"), vmem_limit_bytes=VMEM_LIMIT),
        name="mla_attn",
    )(qt, qt, kn, kr, vt)


def _diff_proj_kernel(h_ref, cos_ref, sin_ref, cos_t_ref, sin_t_ref, wk_ref, wqvt_ref, qt_ref, k_ref, vt_ref,
                      *, q_scale):
    hb = h_ref[0].astype(BF16)
    v_base = DIFF_HEADS * LANES
    qt = _nt_dot(wqvt_ref[0:v_base, :], hb)
    k = _dot(hb, wk_ref[...])
    vt = _nt_dot(wqvt_ref[v_base:2 * v_base, :], hb)
    cos = cos_ref[...]
    sin = sin_ref[...]
    cos_t = cos_t_ref[...]
    sin_t = sin_t_ref[...]
    for h in range(DIFF_HEADS):
        lo, hi = h * LANES, (h + 1) * LANES
        qt_ref[0, lo:hi, :] = (_rope_rows(qt[lo:hi], cos_t, sin_t) * q_scale).astype(BF16)
    for h in range(DIFF_HEADS):
        lo, hi = h * LANES, (h + 1) * LANES
        k_ref[0, :, lo:hi] = _rope_chunk(k[:, lo:hi], cos, sin).astype(BF16)
    for h in range(DIFF_HEADS):
        lo, hi = h * LANES, (h + 1) * LANES
        _store_vt(vt_ref, h, vt[lo:hi])


def _diff_proj(h, cos, sin, cos_t, sin_t, wk, wqvt, q_scale):
    B, S, D = h.shape
    ts = TOKEN_TILE
    H = DIFF_HEADS
    W = H * LANES
    out_shape = (
        jax.ShapeDtypeStruct((B, W, S), BF16),
        jax.ShapeDtypeStruct((B, S, W), BF16),
        jax.ShapeDtypeStruct((B, H, S // ts, VT_ROWS, ts), BF16),
    )
    return pl.pallas_call(
        functools.partial(_diff_proj_kernel, q_scale=q_scale),
        grid=(B, S // ts),
        in_specs=[
            pl.BlockSpec((1, ts, D), lambda b, s: (b, s, 0)),
            pl.BlockSpec((ts, LANES), lambda b, s: (s, 0)),
            pl.BlockSpec((ts, LANES), lambda b, s: (s, 0)),
            pl.BlockSpec((LANES, ts), lambda b, s: (0, s)),
            pl.BlockSpec((LANES, ts), lambda b, s: (0, s)),
            _const_spec(wk.shape), _const_spec(wqvt.shape),
        ],
        out_specs=(
            pl.BlockSpec((1, W, ts), lambda b, s: (b, 0, s)),
            pl.BlockSpec((1, ts, W), lambda b, s: (b, s, 0)),
            pl.BlockSpec((1, H, 1, VT_ROWS, ts), lambda b, s: (b, 0, s, 0, 0)),
        ),
        out_shape=out_shape,
        compiler_params=pltpu.CompilerParams(
            dimension_semantics=("parallel", "parallel"), vmem_limit_bytes=VMEM_LIMIT),
        name="diff_proj",
    )(h, cos, sin, cos_t, sin_t, wk, wqvt)


def _diff_attn_kernel(q_ref, qn_ref, k_ref, vt_ref, lq1_ref, lk1_ref, lq2_ref, lk2_ref, subln_ref, o_ref,
                      qmap_ref, s_ref, mx_ref, m_ref, acc_ref, *, lambda_init):
    tq = q_ref.shape[2]
    hps = vt_ref.shape[1]
    qi = pl.program_id(2)
    feature = lax.broadcasted_iota(jnp.int32, (LANES, tq), 0)

    def one_map(ref, h, c):
        q = ref[0, h * LANES:(h + 1) * LANES, :]
        keep = (feature < DIFF_HEAD_DIM) if c == 0 else (feature >= DIFF_HEAD_DIM)
        return jnp.where(keep, q, jnp.zeros_like(q))

    for i in range(2 * hps):
        qmap_ref[i] = one_map(q_ref, i // 2, i % 2)
    streams = [(lambda i=i: qmap_ref[i],
                lambda i=i: one_map(qn_ref, i // 2, i % 2),
                lambda j, h=i // 2: k_ref[0, pl.ds(pl.multiple_of(j * tq, tq), tq), h * LANES:(h + 1) * LANES],
                lambda j, h=i // 2: vt_ref[0, h, j]) for i in range(2 * hps)]
    _flash_loop(streams, qi, s_ref, mx_ref, m_ref, acc_ref)
    lam = (jnp.exp(jnp.sum(lq1_ref[...] * lk1_ref[...], axis=-1, keepdims=True))
           - jnp.exp(jnp.sum(lq2_ref[...] * lk2_ref[...], axis=-1, keepdims=True)) + lambda_init)
    gain = subln_ref[...] * (1.0 - lambda_init)
    for h in range(hps):
        o = (_normalised(acc_ref, 2 * h) - lam * _normalised(acc_ref, 2 * h + 1)).T
        o = _rms(o, gain)
        o_ref[0, :, h * LANES:(h + 1) * LANES] = o.astype(BF16)


def _diff_attn(qt, k, vt, lq1, lk1, lq2, lk2, subln, lambda_init):
    B, W, S = qt.shape
    H = DIFF_HEADS
    tq = ATTN_TILE
    hps = DIFF_HEADS_PER_STEP
    nkv = vt.shape[2]
    nq = S // tq
    small = lambda a: _const_spec(a.shape)
    return pl.pallas_call(
        functools.partial(_diff_attn_kernel, lambda_init=lambda_init),
        grid=(B, H // hps, nq),
        in_specs=[
            pl.BlockSpec((1, hps * LANES, tq), lambda b, h, i: (b, h, i)),
            pl.BlockSpec((1, hps * LANES, tq), lambda b, h, i: (b, h, jnp.minimum(i + 1, nq - 1))),
            pl.BlockSpec((1, S, hps * LANES), lambda b, h, i: (b, 0, h)),
            pl.BlockSpec((1, hps, nkv, VT_ROWS, tq), lambda b, h, i: (b, h, 0, 0, 0)),
            small(lq1), small(lk1), small(lq2), small(lk2), small(subln),
        ],
        out_specs=pl.BlockSpec((1, tq, hps * LANES), lambda b, h, i: (b, i, h)),
        out_shape=jax.ShapeDtypeStruct((B, S, H * LANES), BF16),
        scratch_shapes=[
            pltpu.VMEM((2 * hps, LANES, tq), BF16),
            pltpu.VMEM((2 * hps, tq, tq), F32),
            pltpu.VMEM((2 * hps, 1, tq), F32),
            pltpu.VMEM((2 * hps, 1, tq), F32),
            pltpu.VMEM((2 * hps, VT_ROWS, tq), F32),
        ],
        compiler_params=pltpu.CompilerParams(
            dimension_semantics=("parallel", "parallel", "arbitrary"), vmem_limit_bytes=VMEM_LIMIT),
        name="diff_attn",
    )(qt, qt, k, vt, lq1, lk1, lq2, lk2, subln)


def _zero_after(v):
    bits = pltpu.bitcast(v[0:8, 0:LANES], jnp.int32)
    zero = lax.shift_right_logical(lax.shift_right_logical(bits, 16), 16)
    return zero[0:1, 0:1].astype(F32)


def _post_kernel(x0_ref, o0_ref, xn_ref, on_ref, wo_ref, g1_ref, b1_ref, wgu_ref, wd_ref, g2_ref, b2_ref, out_ref,
                 h1_ref, resid_prev_ref, *, d_ff):
    ts = h1_ref.shape[0]

    @pl.when(pl.program_id(0) == 0)
    def _seed():
        a0 = _dot(o0_ref[...], wo_ref[...])
        h1_ref[...] = _layernorm(DEEPNORM_ALPHA * x0_ref[...] + a0, g1_ref[...], b1_ref[...])
        resid_prev_ref[...] = jnp.zeros(resid_prev_ref.shape, F32)

    h1 = h1_ref[...]
    hb = h1.astype(BF16)
    attn_next = _dot(on_ref[...], wo_ref[...])

    def ln2_piece(r, zero):
        out_ref[r, :] = _layernorm(resid_prev_ref[r, :] + zero, g2_ref[...], b2_ref[...])

    def ln1_piece(r, zero):
        h1_ref[r, :] = _layernorm(DEEPNORM_ALPHA * xn_ref[r, :] + attn_next[r, :] + zero, g1_ref[...], b1_ref[...])

    pieces = [(fn, slice(r0, r0 + LN_PIECE_ROWS))
              for r0 in range(0, ts, LN_PIECE_ROWS) for fn in (ln2_piece, ln1_piece)]
    chunk_starts = list(range(0, d_ff, FF_CHUNK))
    anchor_chunks = len(chunk_starts) - 1
    acc = jnp.zeros(h1.shape, F32)
    done = 0
    for c, lo in enumerate(chunk_starts):
        hi = min(lo + FF_CHUNK, d_ff)
        g = _dot(hb, wgu_ref[:, lo:hi])
        u = _dot(hb, wgu_ref[:, d_ff + lo:d_ff + hi])
        act = (g * jax.nn.sigmoid(g) * u).astype(BF16)
        acc = acc + _dot(act, wd_ref[lo:hi, :])
        if c < anchor_chunks:
            target = (len(pieces) * (c + 1)) // anchor_chunks
            zero = _zero_after(acc)
            for fn, r in pieces[done:target]:
                fn(r, zero)
            done = target
    resid_prev_ref[...] = DEEPNORM_ALPHA * h1 + acc


def _post(x2d, o2d, wo, g1, b1, wgu, wd, g2, b2):
    T, D = x2d.shape
    ts = TOKEN_TILE
    nt = T // ts
    d_ff = wd.shape[0]
    do = o2d.shape[1]
    first = lambda t: (0, 0)
    following = lambda t: (jnp.minimum(t + 1, nt - 1), 0)
    return pl.pallas_call(
        functools.partial(_post_kernel, d_ff=d_ff),
        grid=(nt + 1,),
        in_specs=[
            pl.BlockSpec((ts, D), first, pipeline_mode=pl.Buffered(1)),
            pl.BlockSpec((ts, do), first, pipeline_mode=pl.Buffered(1)),
            pl.BlockSpec((ts, D), following),
            pl.BlockSpec((ts, do), following),
            _const_spec(wo.shape, True), _const_spec(g1.shape), _const_spec(b1.shape),
            _const_spec(wgu.shape, True), _const_spec(wd.shape, True),
            _const_spec(g2.shape), _const_spec(b2.shape),
        ],
        out_specs=pl.BlockSpec((ts, D), lambda t: (jnp.maximum(t - 1, 0), 0)),
        out_shape=jax.ShapeDtypeStruct((T, D), F32),
        scratch_shapes=[pltpu.VMEM((ts, D), F32), pltpu.VMEM((ts, D), F32)],
        compiler_params=pltpu.CompilerParams(
            dimension_semantics=("arbitrary",), vmem_limit_bytes=VMEM_LIMIT),
        name="post",
    )(x2d, o2d, x2d, o2d, wo, g1, b1, wgu, wd, g2, b2)


def _rope_tables(seq_len, dim):
    inv = 1.0 / (ROPE_THETA ** (jnp.arange(0, dim, 2, dtype=F32) / dim))
    ang = jnp.arange(seq_len, dtype=F32)[:, None] * inv[None, :]
    cos = jnp.cos(ang)
    sin = jnp.sin(ang)
    return jnp.concatenate([cos, cos], axis=-1), jnp.concatenate([-sin, sin], axis=-1)


def kernel(x, mla_w_dq, mla_q_norm, mla_w_uq, mla_w_dkv, mla_kv_norm, mla_w_ukv, mla_w_o, kv_w, diff_w_q, diff_lq1, diff_lk1, diff_lq2, diff_lk2, diff_subln, diff_w_o, ln1_g, ln1_b, ln2_g, ln2_b, ffn_w_gate_up, ffn_w_down):
    B, S, D = x.shape
    H = MLA_HEADS
    assert S % ATTN_TILE == 0 and S % TOKEN_TILE == 0 and TOKEN_TILE == ATTN_TILE
    assert MLA_ROPE == DIFF_HEAD_DIM == 64 and MLA_NOPE == MLA_V == LANES and 2 * DIFF_HEAD_DIM == LANES
    assert MLA_HEADS % MLA_HEADS_PER_STEP == 0 and DIFF_HEADS % DIFF_HEADS_PER_STEP == 0

    cos64, sin64 = _rope_tables(S, MLA_ROPE)
    zeros64 = jnp.zeros_like(cos64)
    cos_mla = jnp.concatenate([cos64, zeros64], axis=-1)
    sin_mla = jnp.concatenate([sin64, zeros64], axis=-1)
    cos_diff = jnp.concatenate([cos64, cos64], axis=-1)
    sin_diff = jnp.concatenate([sin64, sin64], axis=-1)
    cos_mla_t, sin_mla_t, cos_diff_t, sin_diff_t = cos_mla.T, sin_mla.T, cos_diff.T, sin_diff.T

    row = lambda v: v.reshape(1, -1).astype(F32)

    w_dkv = mla_w_dkv[0]
    w_in = jnp.concatenate(
        [mla_w_dq[0], w_dkv[:, :MLA_KV_LORA], w_dkv[:, MLA_KV_LORA:],
         jnp.zeros((D, LANES - MLA_ROPE), F32)], axis=1).astype(BF16)
    w_uq = mla_w_uq[0].reshape(MLA_Q_LORA, H, MLA_NOPE + MLA_ROPE)
    w_uq_rope = jnp.pad(w_uq[:, :, MLA_NOPE:], ((0, 0), (0, 0), (0, LANES - MLA_ROPE)))
    w_uqt = jnp.concatenate(
        [w_uq[:, :, :MLA_NOPE].reshape(MLA_Q_LORA, H * MLA_NOPE),
         w_uq_rope.reshape(MLA_Q_LORA, H * LANES)], axis=1).T.astype(BF16)
    w_ukv = mla_w_ukv[0].reshape(MLA_KV_LORA, H, MLA_NOPE + MLA_V)
    w_uk = w_ukv[:, :, :MLA_NOPE].reshape(MLA_KV_LORA, H * MLA_NOPE).astype(BF16)
    w_uvt = w_ukv[:, :, MLA_NOPE:].reshape(MLA_KV_LORA, H * MLA_V).T.astype(BF16)

    mla_scale = (MLA_NOPE + MLA_ROPE) ** -0.5 * LOG2E
    qt, kn, kr, vt = _mla_proj(x, cos_mla, sin_mla, cos_mla_t, sin_mla_t, w_in, row(mla_q_norm[0]),
                          row(mla_kv_norm[0]), w_uqt, w_uk, w_uvt, mla_scale)
    o = _mla_attn(qt, kn, kr, vt)

    x2d = x.reshape(B * S, D)
    w_gate_up = ffn_w_gate_up.astype(BF16)
    w_down = ffn_w_down.astype(BF16)
    h = _post(x2d, o.reshape(B * S, -1), mla_w_o[0].astype(BF16), row(ln1_g[0]), row(ln1_b[0]),
              w_gate_up[0], w_down[0], row(ln2_g[0]), row(ln2_b[0]))

    qk_width = DIFF_HEADS * 2 * DIFF_HEAD_DIM
    diff_scale = DIFF_HEAD_DIM ** -0.5 * LOG2E
    lambda_init = 0.8 - 0.6 * math.exp(-0.3 * 1)
    w_qvt = jnp.concatenate([diff_w_q[0], kv_w[:, qk_width:]], axis=1).T.astype(BF16)
    dqt, dk, dvt = _diff_proj(h.reshape(B, S, D), cos_diff, sin_diff, cos_diff_t, sin_diff_t,
                              kv_w[:, :qk_width].astype(BF16), w_qvt, diff_scale)
    do = _diff_attn(dqt, dk, dvt, row(diff_lq1[0]), row(diff_lk1[0]), row(diff_lq2[0]), row(diff_lk2[0]),
                    row(diff_subln[0]), lambda_init)
    out = _post(h, do.reshape(B * S, -1), diff_w_o[0].astype(BF16), row(ln1_g[1]), row(ln1_b[1]),
                w_gate_up[1], w_down[1], row(ln2_g[1]), row(ln2_b[1]))
    return out.reshape(B, S, D)
```

```python
import functools
import math

import jax
import jax.numpy as jnp
from jax import lax
from jax.experimental import pallas as pl
from jax.experimental.pallas import tpu as pltpu

F32 = jnp.float32
BF16 = jnp.bfloat16

MLA_HEADS = 8
MLA_NOPE = 128
MLA_ROPE = 64
MLA_V = 128
MLA_Q_LORA = 384
MLA_KV_LORA = 256
DIFF_HEADS = 8
DIFF_HEAD_DIM = 64
ROPE_THETA = 10000.0
DEPTH = 2
DEEPNORM_ALPHA = (2.0 * DEPTH) ** 0.25
LN_EPS = 1e-5
RMS_EPS = 1e-6
LOG2E = math.log2(math.e)
NEG_BIG = -1e30

LANES = 128
BF16_SUBLANES = 16
TOKEN_TILE = 512
ATTN_TILE = 512
MLA_HEADS_PER_STEP = 4
DIFF_HEADS_PER_STEP = 4
FF_CHUNK = 256
LN_PIECE_ROWS = 64
VMEM_LIMIT = 56 * 1024 * 1024
VT_ROWS = LANES + BF16_SUBLANES


def _nt_dot(a, b):
    return lax.dot_general(a, b, (((1,), (1,)), ((), ())), preferred_element_type=F32)


def _dot(a, b):
    return jnp.dot(a, b, preferred_element_type=F32)


def _tn_dot(a, b):
    return lax.dot_general(a, b, (((0,), (0,)), ((), ())), preferred_element_type=F32)


def _rope_chunk(x, cos, sin_signed):
    lane = lax.broadcasted_iota(jnp.int32, x.shape, 1)
    first_half = (lane % 64) < 32
    rot = jnp.where(first_half, pltpu.roll(x, LANES - 32, 1), pltpu.roll(x, 32, 1))
    return x * cos + rot * sin_signed


def _rope_rows(x, cos_t, sin_signed_t):
    rot = jnp.concatenate([x[32:64], x[0:32], x[96:128], x[64:96]], axis=0)
    return x * cos_t + rot * sin_signed_t


def _rms(x, g):
    return x * lax.rsqrt(jnp.mean(jnp.square(x), axis=-1, keepdims=True) + RMS_EPS) * g


def _layernorm(x, g, b):
    mu = jnp.mean(x, axis=-1, keepdims=True)
    xc = x - mu
    var = jnp.mean(jnp.square(xc), axis=-1, keepdims=True)
    return xc * lax.rsqrt(var + LN_EPS) * g + b


def _const_spec(shape, single_buffer=False):
    nd = len(shape)
    kw = {"pipeline_mode": pl.Buffered(1)} if single_buffer else {}
    return pl.BlockSpec(shape, lambda *_: (0,) * nd, **kw)


def _store_vt(vt_ref, h, vt):
    ts = vt.shape[1]
    vt_ref[0, h, 0, 0:LANES, :] = vt.astype(BF16)
    row = lax.broadcasted_iota(jnp.int32, (BF16_SUBLANES, ts), 0)
    vt_ref[0, h, 0, LANES:VT_ROWS, :] = jnp.where(row == 0, 1.0, 0.0).astype(BF16)


def _mla_proj_kernel(x_ref, cos_ref, sin_ref, cos_t_ref, sin_t_ref, w_in_ref, qn_ref, kvn_ref, w_uqt_ref,
                     w_uk_ref, w_uvt_ref, qt_ref, kn_ref, kr_ref, vt_ref, *, q_scale):
    xb = x_ref[0].astype(BF16)
    y = _dot(xb, w_in_ref[...])
    cq = _rms(y[:, :MLA_Q_LORA], qn_ref[...])
    c = _rms(y[:, MLA_Q_LORA:MLA_Q_LORA + MLA_KV_LORA], kvn_ref[...])
    kr_ref[0] = _rope_chunk(y[:, MLA_Q_LORA + MLA_KV_LORA:], cos_ref[...], sin_ref[...]).astype(BF16)
    cos_t = cos_t_ref[...]
    sin_t = sin_t_ref[...]
    qt = _nt_dot(w_uqt_ref[...], cq.astype(BF16))
    cb = c.astype(BF16)
    k_nope = _dot(cb, w_uk_ref[...])
    vt = _nt_dot(w_uvt_ref[...], cb)
    rope_base = MLA_HEADS * MLA_NOPE
    for h in range(MLA_HEADS):
        lo, hi = h * LANES, (h + 1) * LANES
        qt_ref[0, h, 0:LANES, :] = (qt[lo:hi] * q_scale).astype(BF16)
        qt_ref[0, h, LANES:2 * LANES, :] = (
            _rope_rows(qt[rope_base + lo:rope_base + hi], cos_t, sin_t) * q_scale).astype(BF16)
        kn_ref[0, h] = k_nope[:, lo:hi].astype(BF16)
        _store_vt(vt_ref, h, vt[lo:hi, :])


def _mla_proj(x, cos, sin, cos_t, sin_t, w_in, qn, kvn, w_uqt, w_uk, w_uvt, q_scale):
    B, S, D = x.shape
    ts = TOKEN_TILE
    H = MLA_HEADS
    out_shape = (
        jax.ShapeDtypeStruct((B, H, 2 * LANES, S), BF16),
        jax.ShapeDtypeStruct((B, H, S, LANES), BF16),
        jax.ShapeDtypeStruct((B, S, LANES), BF16),
        jax.ShapeDtypeStruct((B, H, S // ts, VT_ROWS, ts), BF16),
    )
    return pl.pallas_call(
        functools.partial(_mla_proj_kernel, q_scale=q_scale),
        grid=(B, S // ts),
        in_specs=[
            pl.BlockSpec((1, ts, D), lambda b, s: (b, s, 0)),
            pl.BlockSpec((ts, LANES), lambda b, s: (s, 0)),
            pl.BlockSpec((ts, LANES), lambda b, s: (s, 0)),
            pl.BlockSpec((LANES, ts), lambda b, s: (0, s)),
            pl.BlockSpec((LANES, ts), lambda b, s: (0, s)),
            _const_spec(w_in.shape), _const_spec(qn.shape), _const_spec(kvn.shape),
            _const_spec(w_uqt.shape), _const_spec(w_uk.shape), _const_spec(w_uvt.shape),
        ],
        out_specs=(
            pl.BlockSpec((1, H, 2 * LANES, ts), lambda b, s: (b, 0, 0, s)),
            pl.BlockSpec((1, H, ts, LANES), lambda b, s: (b, 0, s, 0)),
            pl.BlockSpec((1, ts, LANES), lambda b, s: (b, s, 0)),
            pl.BlockSpec((1, H, 1, VT_ROWS, ts), lambda b, s: (b, 0, s, 0, 0)),
        ),
        out_shape=out_shape,
        compiler_params=pltpu.CompilerParams(
            dimension_semantics=("parallel", "parallel"), vmem_limit_bytes=VMEM_LIMIT),
        name="mla_proj",
    )(x, cos, sin, cos_t, sin_t, w_in, qn, kvn, w_uqt, w_uk, w_uvt)


def _flash_loop(streams, qi, s_ref, mx_ref, m_ref, acc_ref):
    @pl.when(qi == 0)
    def _seed():
        for i, (q_cur, _, k_block, _) in enumerate(streams):
            s_ref[i] = _dot(k_block(0), q_cur())

    m_ref[...] = jnp.full(m_ref.shape, NEG_BIG, F32)
    acc_ref[...] = jnp.zeros(acc_ref.shape, F32)
    tq = s_ref.shape[2]
    half = tq // 2

    def produce(i, k_blk, q):
        s_new = _dot(k_blk, q)
        s_ref[i] = s_new
        mx_ref[i] = jnp.max(s_new, axis=0, keepdims=True)

    def full_step(j):
        for i, (q_cur, _, k_block, vt_block) in enumerate(streams):
            m_prev = m_ref[i]
            m_new = jnp.maximum(m_prev, mx_ref[i])
            alpha = jnp.exp2(m_prev - m_new)
            p = jnp.exp2(s_ref[i] - m_new).astype(BF16)
            m_ref[i] = m_new
            produce(i, k_block(j + 1), q_cur())
            acc_ref[i] = acc_ref[i] * alpha + _dot(vt_block(j), p)

    def diagonal_step(j):
        row = lax.broadcasted_iota(jnp.int32, (half, half), 0)
        col = lax.broadcasted_iota(jnp.int32, (half, half), 1)
        visible = row <= col
        for i, (_, q_next, k_block, vt_block) in enumerate(streams):
            s_tl = jnp.where(visible, s_ref[i, 0:half, 0:half], NEG_BIG)
            s_tr = s_ref[i, 0:half, half:tq]
            s_br = jnp.where(visible, s_ref[i, half:tq, half:tq], NEG_BIG)
            m_cur = jnp.concatenate(
                [jnp.max(s_tl, axis=0, keepdims=True),
                 jnp.maximum(jnp.max(s_tr, axis=0, keepdims=True), jnp.max(s_br, axis=0, keepdims=True))], axis=1)
            m_prev = m_ref[i]
            m_new = jnp.maximum(m_prev, m_cur)
            alpha = jnp.exp2(m_prev - m_new)
            m_ref[i] = m_new
            p_tl = jnp.exp2(s_tl - m_new[:, 0:half]).astype(BF16)
            p_tr = jnp.exp2(s_tr - m_new[:, half:tq]).astype(BF16)
            p_br = jnp.exp2(s_br - m_new[:, half:tq]).astype(BF16)
            produce(i, k_block(0), q_next())
            vt = vt_block(j)
            acc_ref[i, :, 0:half] = acc_ref[i, :, 0:half] * alpha[:, 0:half] + _dot(vt[:, 0:half], p_tl)
            acc_ref[i, :, half:tq] = (acc_ref[i, :, half:tq] * alpha[:, half:tq]
                                      + _dot(vt[:, 0:half], p_tr) + _dot(vt[:, half:tq], p_br))

    def body(pair, carry):
        full_step(2 * pair)
        full_step(2 * pair + 1)
        return carry

    lax.fori_loop(0, qi // 2, body, 0)

    @pl.when(qi % 2 == 1)
    def _odd_block():
        full_step(qi - 1)

    diagonal_step(qi)


def _normalised(acc_ref, i):
    return acc_ref[i, 0:LANES, :] * (1.0 / acc_ref[i, LANES:LANES + 1, :])


def _mla_attn_kernel(q_ref, qn_ref, kn_ref, kr_ref, vt_ref, o_ref, s_ref, mx_ref, m_ref, acc_ref):
    hps, tq = q_ref.shape[1], q_ref.shape[3]
    qi = pl.program_id(2)

    def key_block(j, h):
        rows = pl.ds(pl.multiple_of(j * tq, tq), tq)
        return jnp.concatenate([kn_ref[0, h, rows, :], kr_ref[0, rows, :]], axis=1)

    streams = [(lambda h=h: q_ref[0, h],
                lambda h=h: qn_ref[0, h],
                lambda j, h=h: key_block(j, h),
                lambda j, h=h: vt_ref[0, h, j]) for h in range(hps)]
    _flash_loop(streams, qi, s_ref, mx_ref, m_ref, acc_ref)
    for h in range(hps):
        o_ref[0, h * LANES:(h + 1) * LANES, :] = _normalised(acc_ref, h).astype(BF16)


def _mla_attn(qt, kn, kr, vt):
    B, H, Dk, S = qt.shape
    tq = ATTN_TILE
    hps = MLA_HEADS_PER_STEP
    nkv = vt.shape[2]
    nq = S // tq
    return pl.pallas_call(
        _mla_attn_kernel,
        grid=(B, H // hps, nq),
        in_specs=[
            pl.BlockSpec((1, hps, Dk, tq), lambda b, h, i: (b, h, 0, i)),
            pl.BlockSpec((1, hps, Dk, tq), lambda b, h, i: (b, h, 0, jnp.minimum(i + 1, nq - 1))),
            pl.BlockSpec((1, hps, S, LANES), lambda b, h, i: (b, h, 0, 0)),
            pl.BlockSpec((1, S, LANES), lambda b, h, i: (b, 0, 0)),
            pl.BlockSpec((1, hps, nkv, VT_ROWS, tq), lambda b, h, i: (b, h, 0, 0, 0)),
        ],
        out_specs=pl.BlockSpec((1, hps * LANES, tq), lambda b, h, i: (b, h, i)),
        out_shape=jax.ShapeDtypeStruct((B, H * LANES, S), BF16),
        scratch_shapes=[pltpu.VMEM((hps, tq, tq), F32), pltpu.VMEM((hps, 1, tq), F32), pltpu.VMEM((hps, 1, tq), F32),
                        pltpu.VMEM((hps, VT_ROWS, tq), F32)],
        compiler_params=pltpu.CompilerParams(
            dimension_semantics=("parallel", "parallel", "arbitrary"), vmem_limit_bytes=VMEM_LIMIT),
        name="mla_attn",
    )(qt, qt, kn, kr, vt)


def _diff_proj_kernel(h_ref, cos_ref, sin_ref, cos_t_ref, sin_t_ref, wk_ref, wqvt_ref, qt_ref, k_ref, vt_ref,
                      *, q_scale):
    hb = h_ref[0].astype(BF16)
    v_base = DIFF_HEADS * LANES
    qt = _nt_dot(wqvt_ref[0:v_base, :], hb)
    k = _dot(hb, wk_ref[...])
    vt = _nt_dot(wqvt_ref[v_base:2 * v_base, :], hb)
    cos = cos_ref[...]
    sin = sin_ref[...]
    cos_t = cos_t_ref[...]
    sin_t = sin_t_ref[...]
    for h in range(DIFF_HEADS):
        lo, hi = h * LANES, (h + 1) * LANES
        qt_ref[0, lo:hi, :] = (_rope_rows(qt[lo:hi], cos_t, sin_t) * q_scale).astype(BF16)
    for h in range(DIFF_HEADS):
        lo, hi = h * LANES, (h + 1) * LANES
        k_ref[0, :, lo:hi] = _rope_chunk(k[:, lo:hi], cos, sin).astype(BF16)
    for h in range(DIFF_HEADS):
        lo, hi = h * LANES, (h + 1) * LANES
        _store_vt(vt_ref, h, vt[lo:hi])


def _diff_proj(h, cos, sin, cos_t, sin_t, wk, wqvt, q_scale):
    B, S, D = h.shape
    ts = TOKEN_TILE
    H = DIFF_HEADS
    W = H * LANES
    out_shape = (
        jax.ShapeDtypeStruct((B, W, S), BF16),
        jax.ShapeDtypeStruct((B, S, W), BF16),
        jax.ShapeDtypeStruct((B, H, S // ts, VT_ROWS, ts), BF16),
    )
    return pl.pallas_call(
        functools.partial(_diff_proj_kernel, q_scale=q_scale),
        grid=(B, S // ts),
        in_specs=[
            pl.BlockSpec((1, ts, D), lambda b, s: (b, s, 0)),
            pl.BlockSpec((ts, LANES), lambda b, s: (s, 0)),
            pl.BlockSpec((ts, LANES), lambda b, s: (s, 0)),
            pl.BlockSpec((LANES, ts), lambda b, s: (0, s)),
            pl.BlockSpec((LANES, ts), lambda b, s: (0, s)),
            _const_spec(wk.shape), _const_spec(wqvt.shape),
        ],
        out_specs=(
            pl.BlockSpec((1, W, ts), lambda b, s: (b, 0, s)),
            pl.BlockSpec((1, ts, W), lambda b, s: (b, s, 0)),
            pl.BlockSpec((1, H, 1, VT_ROWS, ts), lambda b, s: (b, 0, s, 0, 0)),
        ),
        out_shape=out_shape,
        compiler_params=pltpu.CompilerParams(
            dimension_semantics=("parallel", "parallel"), vmem_limit_bytes=VMEM_LIMIT),
        name="diff_proj",
    )(h, cos, sin, cos_t, sin_t, wk, wqvt)


def _diff_attn_kernel(q_ref, qn_ref, k_ref, vt_ref, lq1_ref, lk1_ref, lq2_ref, lk2_ref, subln_ref, o_ref,
                      qmap_ref, s_ref, mx_ref, m_ref, acc_ref, *, lambda_init):
    tq = q_ref.shape[2]
    hps = vt_ref.shape[1]
    qi = pl.program_id(2)
    feature = lax.broadcasted_iota(jnp.int32, (LANES, tq), 0)

    def one_map(ref, h, c):
        q = ref[0, h * LANES:(h + 1) * LANES, :]
        keep = (feature < DIFF_HEAD_DIM) if c == 0 else (feature >= DIFF_HEAD_DIM)
        return jnp.where(keep, q, jnp.zeros_like(q))

    for i in range(2 * hps):
        qmap_ref[i] = one_map(q_ref, i // 2, i % 2)
    streams = [(lambda i=i: qmap_ref[i],
                lambda i=i: one_map(qn_ref, i // 2, i % 2),
                lambda j, h=i // 2: k_ref[0, pl.ds(pl.multiple_of(j * tq, tq), tq), h * LANES:(h + 1) * LANES],
                lambda j, h=i // 2: vt_ref[0, h, j]) for i in range(2 * hps)]
    _flash_loop(streams, qi, s_ref, mx_ref, m_ref, acc_ref)
    lam = (jnp.exp(jnp.sum(lq1_ref[...] * lk1_ref[...], axis=-1, keepdims=True))
           - jnp.exp(jnp.sum(lq2_ref[...] * lk2_ref[...], axis=-1, keepdims=True)) + lambda_init)
    gain = subln_ref[...] * (1.0 - lambda_init)
    for h in range(hps):
        o_t = _normalised(acc_ref, 2 * h) - lam * _normalised(acc_ref, 2 * h + 1)
        o_t = o_t * lax.rsqrt(jnp.mean(jnp.square(o_t), axis=0, keepdims=True) + RMS_EPS) * gain
        o_ref[0, h * LANES:(h + 1) * LANES, :] = o_t.astype(BF16)


def _diff_attn(qt, k, vt, lq1, lk1, lq2, lk2, subln, lambda_init):
    B, W, S = qt.shape
    H = DIFF_HEADS
    tq = ATTN_TILE
    hps = DIFF_HEADS_PER_STEP
    nkv = vt.shape[2]
    nq = S // tq
    small = lambda a: _const_spec(a.shape)
    return pl.pallas_call(
        functools.partial(_diff_attn_kernel, lambda_init=lambda_init),
        grid=(B, H // hps, nq),
        in_specs=[
            pl.BlockSpec((1, hps * LANES, tq), lambda b, h, i: (b, h, i)),
            pl.BlockSpec((1, hps * LANES, tq), lambda b, h, i: (b, h, jnp.minimum(i + 1, nq - 1))),
            pl.BlockSpec((1, S, hps * LANES), lambda b, h, i: (b, 0, h)),
            pl.BlockSpec((1, hps, nkv, VT_ROWS, tq), lambda b, h, i: (b, h, 0, 0, 0)),
            small(lq1), small(lk1), small(lq2), small(lk2), small(subln),
        ],
        out_specs=pl.BlockSpec((1, hps * LANES, tq), lambda b, h, i: (b, h, i)),
        out_shape=jax.ShapeDtypeStruct((B, H * LANES, S), BF16),
        scratch_shapes=[
            pltpu.VMEM((2 * hps, LANES, tq), BF16),
            pltpu.VMEM((2 * hps, tq, tq), F32),
            pltpu.VMEM((2 * hps, 1, tq), F32),
            pltpu.VMEM((2 * hps, 1, tq), F32),
            pltpu.VMEM((2 * hps, VT_ROWS, tq), F32),
        ],
        compiler_params=pltpu.CompilerParams(
            dimension_semantics=("parallel", "parallel", "arbitrary"), vmem_limit_bytes=VMEM_LIMIT),
        name="diff_attn",
    )(qt, qt, k, vt, lq1, lk1, lq2, lk2, subln)


def _zero_after(v):
    bits = pltpu.bitcast(v[0:8, 0:LANES], jnp.int32)
    zero = lax.shift_right_logical(lax.shift_right_logical(bits, 16), 16)
    return zero[0:1, 0:1].astype(F32)


def _post_kernel(x0_ref, o0_ref, xn_ref, on_ref, wo_ref, g1_ref, b1_ref, wgu_ref, wd_ref, g2_ref, b2_ref, out_ref,
                 h1_ref, resid_prev_ref, *, d_ff):
    ts = h1_ref.shape[0]

    @pl.when(pl.program_id(0) == 0)
    def _seed():
        a0 = _tn_dot(o0_ref[0], wo_ref[...])
        h1_ref[...] = _layernorm(DEEPNORM_ALPHA * x0_ref[...] + a0, g1_ref[...], b1_ref[...])
        resid_prev_ref[...] = jnp.zeros(resid_prev_ref.shape, F32)

    h1 = h1_ref[...]
    hb = h1.astype(BF16)
    attn_next = _tn_dot(on_ref[0], wo_ref[...])

    def ln2_piece(r, zero):
        out_ref[r, :] = _layernorm(resid_prev_ref[r, :] + zero, g2_ref[...], b2_ref[...])

    def ln1_piece(r, zero):
        h1_ref[r, :] = _layernorm(DEEPNORM_ALPHA * xn_ref[r, :] + attn_next[r, :] + zero, g1_ref[...], b1_ref[...])

    pieces = [(fn, slice(r0, r0 + LN_PIECE_ROWS))
              for r0 in range(0, ts, LN_PIECE_ROWS) for fn in (ln2_piece, ln1_piece)]
    chunk_starts = list(range(0, d_ff, FF_CHUNK))
    anchor_chunks = len(chunk_starts) - 1
    acc = jnp.zeros(h1.shape, F32)
    done = 0
    for c, lo in enumerate(chunk_starts):
        hi = min(lo + FF_CHUNK, d_ff)
        g = _dot(hb, wgu_ref[:, lo:hi])
        u = _dot(hb, wgu_ref[:, d_ff + lo:d_ff + hi])
        act = (g * jax.nn.sigmoid(g) * u).astype(BF16)
        acc = acc + _dot(act, wd_ref[lo:hi, :])
        if c < anchor_chunks:
            target = (len(pieces) * (c + 1)) // anchor_chunks
            zero = _zero_after(acc)
            for fn, r in pieces[done:target]:
                fn(r, zero)
            done = target
    resid_prev_ref[...] = DEEPNORM_ALPHA * h1 + acc


def _post(x2d, o_t, wo, g1, b1, wgu, wd, g2, b2):
    T, D = x2d.shape
    B, do, S = o_t.shape
    ts = TOKEN_TILE
    nt = T // ts
    tiles_per_seq = S // ts
    d_ff = wd.shape[0]
    following = lambda t: jnp.minimum(t + 1, nt - 1)
    return pl.pallas_call(
        functools.partial(_post_kernel, d_ff=d_ff),
        grid=(nt + 1,),
        in_specs=[
            pl.BlockSpec((ts, D), lambda t: (0, 0), pipeline_mode=pl.Buffered(1)),
            pl.BlockSpec((1, do, ts), lambda t: (0, 0, 0), pipeline_mode=pl.Buffered(1)),
            pl.BlockSpec((ts, D), lambda t: (following(t), 0)),
            pl.BlockSpec((1, do, ts), lambda t: (following(t) // tiles_per_seq, 0, following(t) % tiles_per_seq)),
            _const_spec(wo.shape, True), _const_spec(g1.shape), _const_spec(b1.shape),
            _const_spec(wgu.shape, True), _const_spec(wd.shape, True),
            _const_spec(g2.shape), _const_spec(b2.shape),
        ],
        out_specs=pl.BlockSpec((ts, D), lambda t: (jnp.maximum(t - 1, 0), 0)),
        out_shape=jax.ShapeDtypeStruct((T, D), F32),
        scratch_shapes=[pltpu.VMEM((ts, D), F32), pltpu.VMEM((ts, D), F32)],
        compiler_params=pltpu.CompilerParams(
            dimension_semantics=("arbitrary",), vmem_limit_bytes=VMEM_LIMIT),
        name="post",
    )(x2d, o_t, x2d, o_t, wo, g1, b1, wgu, wd, g2, b2)


def _rope_tables(seq_len, dim):
    inv = 1.0 / (ROPE_THETA ** (jnp.arange(0, dim, 2, dtype=F32) / dim))
    ang = jnp.arange(seq_len, dtype=F32)[:, None] * inv[None, :]
    cos = jnp.cos(ang)
    sin = jnp.sin(ang)
    return jnp.concatenate([cos, cos], axis=-1), jnp.concatenate([-sin, sin], axis=-1)


def kernel(x, mla_w_dq, mla_q_norm, mla_w_uq, mla_w_dkv, mla_kv_norm, mla_w_ukv, mla_w_o, kv_w, diff_w_q, diff_lq1, diff_lk1, diff_lq2, diff_lk2, diff_subln, diff_w_o, ln1_g, ln1_b, ln2_g, ln2_b, ffn_w_gate_up, ffn_w_down):
    B, S, D = x.shape
    H = MLA_HEADS
    assert S % ATTN_TILE == 0 and S % TOKEN_TILE == 0 and TOKEN_TILE == ATTN_TILE
    assert MLA_ROPE == DIFF_HEAD_DIM == 64 and MLA_NOPE == MLA_V == LANES and 2 * DIFF_HEAD_DIM == LANES
    assert MLA_HEADS % MLA_HEADS_PER_STEP == 0 and DIFF_HEADS % DIFF_HEADS_PER_STEP == 0

    cos64, sin64 = _rope_tables(S, MLA_ROPE)
    zeros64 = jnp.zeros_like(cos64)
    cos_mla = jnp.concatenate([cos64, zeros64], axis=-1)
    sin_mla = jnp.concatenate([sin64, zeros64], axis=-1)
    cos_diff = jnp.concatenate([cos64, cos64], axis=-1)
    sin_diff = jnp.concatenate([sin64, sin64], axis=-1)
    cos_mla_t, sin_mla_t, cos_diff_t, sin_diff_t = cos_mla.T, sin_mla.T, cos_diff.T, sin_diff.T

    row = lambda v: v.reshape(1, -1).astype(F32)

    w_dkv = mla_w_dkv[0]
    w_in = jnp.concatenate(
        [mla_w_dq[0], w_dkv[:, :MLA_KV_LORA], w_dkv[:, MLA_KV_LORA:],
         jnp.zeros((D, LANES - MLA_ROPE), F32)], axis=1).astype(BF16)
    w_uq = mla_w_uq[0].reshape(MLA_Q_LORA, H, MLA_NOPE + MLA_ROPE)
    w_uq_rope = jnp.pad(w_uq[:, :, MLA_NOPE:], ((0, 0), (0, 0), (0, LANES - MLA_ROPE)))
    w_uqt = jnp.concatenate(
        [w_uq[:, :, :MLA_NOPE].reshape(MLA_Q_LORA, H * MLA_NOPE),
         w_uq_rope.reshape(MLA_Q_LORA, H * LANES)], axis=1).T.astype(BF16)
    w_ukv = mla_w_ukv[0].reshape(MLA_KV_LORA, H, MLA_NOPE + MLA_V)
    w_uk = w_ukv[:, :, :MLA_NOPE].reshape(MLA_KV_LORA, H * MLA_NOPE).astype(BF16)
    w_uvt = w_ukv[:, :, MLA_NOPE:].reshape(MLA_KV_LORA, H * MLA_V).T.astype(BF16)

    mla_scale = (MLA_NOPE + MLA_ROPE) ** -0.5 * LOG2E
    qt, kn, kr, vt = _mla_proj(x, cos_mla, sin_mla, cos_mla_t, sin_mla_t, w_in, row(mla_q_norm[0]),
                          row(mla_kv_norm[0]), w_uqt, w_uk, w_uvt, mla_scale)
    o = _mla_attn(qt, kn, kr, vt)

    x2d = x.reshape(B * S, D)
    w_gate_up = ffn_w_gate_up.astype(BF16)
    w_down = ffn_w_down.astype(BF16)
    h = _post(x2d, o, mla_w_o[0].astype(BF16), row(ln1_g[0]), row(ln1_b[0]),
              w_gate_up[0], w_down[0], row(ln2_g[0]), row(ln2_b[0]))

    qk_width = DIFF_HEADS * 2 * DIFF_HEAD_DIM
    diff_scale = DIFF_HEAD_DIM ** -0.5 * LOG2E
    lambda_init = 0.8 - 0.6 * math.exp(-0.3 * 1)
    w_qvt = jnp.concatenate([diff_w_q[0], kv_w[:, qk_width:]], axis=1).T.astype(BF16)
    dqt, dk, dvt = _diff_proj(h.reshape(B, S, D), cos_diff, sin_diff, cos_diff_t, sin_diff_t,
                              kv_w[:, :qk_width].astype(BF16), w_qvt, diff_scale)
    do = _diff_attn(dqt, dk, dvt, row(diff_lq1[0]), row(diff_lk1[0]), row(diff_lq2[0]), row(diff_lk2[0]),
                    diff_subln[0].reshape(-1, 1).astype(F32), lambda_init)
    out = _post(h, do, diff_w_o[0].astype(BF16), row(ln1_g[1]), row(ln1_b[1]),
                w_gate_up[1], w_down[1], row(ln2_g[1]), row(ln2_b[1]))
    return out.reshape(B, S, D)
```

```python
import functools
import math

import jax
import jax.numpy as jnp
from jax import lax
from jax.experimental import pallas as pl
from jax.experimental.pallas import tpu as pltpu

F32 = jnp.float32
BF16 = jnp.bfloat16

MLA_HEADS = 8
MLA_NOPE = 128
MLA_ROPE = 64
MLA_V = 128
MLA_Q_LORA = 384
MLA_KV_LORA = 256
DIFF_HEADS = 8
DIFF_HEAD_DIM = 64
ROPE_THETA = 10000.0
DEPTH = 2
DEEPNORM_ALPHA = (2.0 * DEPTH) ** 0.25
LN_EPS = 1e-5
RMS_EPS = 1e-6
LOG2E = math.log2(math.e)
NEG_BIG = -1e30

LANES = 128
BF16_SUBLANES = 16
TOKEN_TILE = 512
ATTN_TILE = 512
MLA_HEADS_PER_STEP = 4
DIFF_HEADS_PER_STEP = 4
FF_CHUNK = 256
LN_PIECE_ROWS = 64
GATE_UP_STAGE_ROWS = 64
DOWN_STAGE_ROWS = 256
VMEM_LIMIT = 56 * 1024 * 1024
VT_ROWS = LANES + BF16_SUBLANES


def _nt_dot(a, b):
    return lax.dot_general(a, b, (((1,), (1,)), ((), ())), preferred_element_type=F32)


def _dot(a, b):
    return jnp.dot(a, b, preferred_element_type=F32)


def _rope_chunk(x, cos, sin_signed):
    lane = lax.broadcasted_iota(jnp.int32, x.shape, 1)
    first_half = (lane % 64) < 32
    rot = jnp.where(first_half, pltpu.roll(x, LANES - 32, 1), pltpu.roll(x, 32, 1))
    return x * cos + rot * sin_signed


def _rope_rows(x, cos_t, sin_signed_t):
    rot = jnp.concatenate([x[32:64], x[0:32], x[96:128], x[64:96]], axis=0)
    return x * cos_t + rot * sin_signed_t


def _rms(x, g):
    return x * lax.rsqrt(jnp.mean(jnp.square(x), axis=-1, keepdims=True) + RMS_EPS) * g


def _layernorm(x, g, b):
    mu = jnp.mean(x, axis=-1, keepdims=True)
    xc = x - mu
    var = jnp.mean(jnp.square(xc), axis=-1, keepdims=True)
    return xc * lax.rsqrt(var + LN_EPS) * g + b


def _const_spec(shape, single_buffer=False):
    nd = len(shape)
    kw = {"pipeline_mode": pl.Buffered(1)} if single_buffer else {}
    return pl.BlockSpec(shape, lambda *_: (0,) * nd, **kw)


def _store_vt(vt_ref, h, vt):
    ts = vt.shape[1]
    vt_ref[0, h, 0, 0:LANES, :] = vt.astype(BF16)
    row = lax.broadcasted_iota(jnp.int32, (BF16_SUBLANES, ts), 0)
    vt_ref[0, h, 0, LANES:VT_ROWS, :] = jnp.where(row == 0, 1.0, 0.0).astype(BF16)


def _mla_proj_kernel(x_ref, cos_ref, sin_ref, cos_t_ref, sin_t_ref, w_in_ref, qn_ref, kvn_ref, w_uqt_ref,
                     w_uk_ref, w_uvt_ref, qt_ref, kn_ref, kr_ref, vt_ref, *, q_scale):
    xb = x_ref[0].astype(BF16)
    y = _dot(xb, w_in_ref[...])
    cq = _rms(y[:, :MLA_Q_LORA], qn_ref[...])
    c = _rms(y[:, MLA_Q_LORA:MLA_Q_LORA + MLA_KV_LORA], kvn_ref[...])
    kr_ref[0] = _rope_chunk(y[:, MLA_Q_LORA + MLA_KV_LORA:], cos_ref[...], sin_ref[...]).astype(BF16)
    cos_t = cos_t_ref[...]
    sin_t = sin_t_ref[...]
    qt = _nt_dot(w_uqt_ref[...], cq.astype(BF16))
    cb = c.astype(BF16)
    k_nope = _dot(cb, w_uk_ref[...])
    vt = _nt_dot(w_uvt_ref[...], cb)
    rope_base = MLA_HEADS * MLA_NOPE
    for h in range(MLA_HEADS):
        lo, hi = h * LANES, (h + 1) * LANES
        qt_ref[0, h, 0:LANES, :] = (qt[lo:hi] * q_scale).astype(BF16)
        qt_ref[0, h, LANES:2 * LANES, :] = (
            _rope_rows(qt[rope_base + lo:rope_base + hi], cos_t, sin_t) * q_scale).astype(BF16)
        kn_ref[0, h] = k_nope[:, lo:hi].astype(BF16)
        _store_vt(vt_ref, h, vt[lo:hi, :])


def _mla_proj(x, cos, sin, cos_t, sin_t, w_in, qn, kvn, w_uqt, w_uk, w_uvt, q_scale):
    B, S, D = x.shape
    ts = TOKEN_TILE
    H = MLA_HEADS
    out_shape = (
        jax.ShapeDtypeStruct((B, H, 2 * LANES, S), BF16),
        jax.ShapeDtypeStruct((B, H, S, LANES), BF16),
        jax.ShapeDtypeStruct((B, S, LANES), BF16),
        jax.ShapeDtypeStruct((B, H, S // ts, VT_ROWS, ts), BF16),
    )
    return pl.pallas_call(
        functools.partial(_mla_proj_kernel, q_scale=q_scale),
        grid=(B, S // ts),
        in_specs=[
            pl.BlockSpec((1, ts, D), lambda b, s: (b, s, 0)),
            pl.BlockSpec((ts, LANES), lambda b, s: (s, 0)),
            pl.BlockSpec((ts, LANES), lambda b, s: (s, 0)),
            pl.BlockSpec((LANES, ts), lambda b, s: (0, s)),
            pl.BlockSpec((LANES, ts), lambda b, s: (0, s)),
            _const_spec(w_in.shape), _const_spec(qn.shape), _const_spec(kvn.shape),
            _const_spec(w_uqt.shape), _const_spec(w_uk.shape), _const_spec(w_uvt.shape),
        ],
        out_specs=(
            pl.BlockSpec((1, H, 2 * LANES, ts), lambda b, s: (b, 0, 0, s)),
            pl.BlockSpec((1, H, ts, LANES), lambda b, s: (b, 0, s, 0)),
            pl.BlockSpec((1, ts, LANES), lambda b, s: (b, s, 0)),
            pl.BlockSpec((1, H, 1, VT_ROWS, ts), lambda b, s: (b, 0, s, 0, 0)),
        ),
        out_shape=out_shape,
        compiler_params=pltpu.CompilerParams(
            dimension_semantics=("parallel", "parallel"), vmem_limit_bytes=VMEM_LIMIT),
        name="mla_proj",
    )(x, cos, sin, cos_t, sin_t, w_in, qn, kvn, w_uqt, w_uk, w_uvt)


def _flash_loop(streams, qi, s_ref, mx_ref, m_ref, acc_ref):
    @pl.when(qi == 0)
    def _seed():
        for i, (q_cur, _, k_block, _) in enumerate(streams):
            s_ref[i] = _dot(k_block(0), q_cur())

    m_ref[...] = jnp.full(m_ref.shape, NEG_BIG, F32)
    acc_ref[...] = jnp.zeros(acc_ref.shape, F32)
    tq = s_ref.shape[2]
    half = tq // 2

    def produce(i, k_blk, q):
        s_new = _dot(k_blk, q)
        s_ref[i] = s_new
        mx_ref[i] = jnp.max(s_new, axis=0, keepdims=True)

    def full_step(j):
        for i, (q_cur, _, k_block, vt_block) in enumerate(streams):
            m_prev = m_ref[i]
            m_new = jnp.maximum(m_prev, mx_ref[i])
            alpha = jnp.exp2(m_prev - m_new)
            p = jnp.exp2(s_ref[i] - m_new).astype(BF16)
            m_ref[i] = m_new
            produce(i, k_block(j + 1), q_cur())
            acc_ref[i] = acc_ref[i] * alpha + _dot(vt_block(j), p)

    def diagonal_step(j):
        row = lax.broadcasted_iota(jnp.int32, (half, half), 0)
        col = lax.broadcasted_iota(jnp.int32, (half, half), 1)
        visible = row <= col
        for i, (_, q_next, k_block, vt_block) in enumerate(streams):
            s_tl = jnp.where(visible, s_ref[i, 0:half, 0:half], NEG_BIG)
            s_tr = s_ref[i, 0:half, half:tq]
            s_br = jnp.where(visible, s_ref[i, half:tq, half:tq], NEG_BIG)
            m_cur = jnp.concatenate(
                [jnp.max(s_tl, axis=0, keepdims=True),
                 jnp.maximum(jnp.max(s_tr, axis=0, keepdims=True), jnp.max(s_br, axis=0, keepdims=True))], axis=1)
            m_prev = m_ref[i]
            m_new = jnp.maximum(m_prev, m_cur)
            alpha = jnp.exp2(m_prev - m_new)
            m_ref[i] = m_new
            p_tl = jnp.exp2(s_tl - m_new[:, 0:half]).astype(BF16)
            p_tr = jnp.exp2(s_tr - m_new[:, half:tq]).astype(BF16)
            p_br = jnp.exp2(s_br - m_new[:, half:tq]).astype(BF16)
            produce(i, k_block(0), q_next())
            vt = vt_block(j)
            acc_ref[i, :, 0:half] = acc_ref[i, :, 0:half] * alpha[:, 0:half] + _dot(vt[:, 0:half], p_tl)
            acc_ref[i, :, half:tq] = (acc_ref[i, :, half:tq] * alpha[:, half:tq]
                                      + _dot(vt[:, 0:half], p_tr) + _dot(vt[:, half:tq], p_br))

    def body(pair, carry):
        full_step(2 * pair)
        full_step(2 * pair + 1)
        return carry

    lax.fori_loop(0, qi // 2, body, 0)

    @pl.when(qi % 2 == 1)
    def _odd_block():
        full_step(qi - 1)

    diagonal_step(qi)


def _normalised(acc_ref, i):
    return acc_ref[i, 0:LANES, :] * (1.0 / acc_ref[i, LANES:LANES + 1, :])


def _mla_attn_kernel(q_ref, qn_ref, kn_ref, kr_ref, vt_ref, o_ref, s_ref, mx_ref, m_ref, acc_ref):
    hps, tq = q_ref.shape[1], q_ref.shape[3]
    qi = pl.program_id(2)

    def key_block(j, h):
        rows = pl.ds(pl.multiple_of(j * tq, tq), tq)
        return jnp.concatenate([kn_ref[0, h, rows, :], kr_ref[0, rows, :]], axis=1)

    streams = [(lambda h=h: q_ref[0, h],
                lambda h=h: qn_ref[0, h],
                lambda j, h=h: key_block(j, h),
                lambda j, h=h: vt_ref[0, h, j]) for h in range(hps)]
    _flash_loop(streams, qi, s_ref, mx_ref, m_ref, acc_ref)
    for h in range(hps):
        o_ref[0, :, h * LANES:(h + 1) * LANES] = _normalised(acc_ref, h).T.astype(BF16)


def _mla_attn(qt, kn, kr, vt):
    B, H, Dk, S = qt.shape
    tq = ATTN_TILE
    hps = MLA_HEADS_PER_STEP
    nkv = vt.shape[2]
    nq = S // tq
    return pl.pallas_call(
        _mla_attn_kernel,
        grid=(B, H // hps, nq),
        in_specs=[
            pl.BlockSpec((1, hps, Dk, tq), lambda b, h, i: (b, h, 0, i)),
            pl.BlockSpec((1, hps, Dk, tq), lambda b, h, i: (b, h, 0, jnp.minimum(i + 1, nq - 1))),
            pl.BlockSpec((1, hps, S, LANES), lambda b, h, i: (b, h, 0, 0)),
            pl.BlockSpec((1, S, LANES), lambda b, h, i: (b, 0, 0)),
            pl.BlockSpec((1, hps, nkv, VT_ROWS, tq), lambda b, h, i: (b, h, 0, 0, 0)),
        ],
        out_specs=pl.BlockSpec((1, tq, hps * LANES), lambda b, h, i: (b, i, h)),
        out_shape=jax.ShapeDtypeStruct((B, S, H * LANES), BF16),
        scratch_shapes=[pltpu.VMEM((hps, tq, tq), F32), pltpu.VMEM((hps, 1, tq), F32), pltpu.VMEM((hps, 1, tq), F32),
                        pltpu.VMEM((hps, VT_ROWS, tq), F32)],
        compiler_params=pltpu.CompilerParams(
            dimension_semantics=("parallel", "parallel", "arbitrary"), vmem_limit_bytes=VMEM_LIMIT),
        name="mla_attn",
    )(qt, qt, kn, kr, vt)


def _diff_proj_kernel(h_ref, cos_ref, sin_ref, cos_t_ref, sin_t_ref, wk_ref, wqvt_ref, qt_ref, k_ref, vt_ref,
                      *, q_scale):
    hb = h_ref[0].astype(BF16)
    v_base = DIFF_HEADS * LANES
    qt = _nt_dot(wqvt_ref[0:v_base, :], hb)
    k = _dot(hb, wk_ref[...])
    vt = _nt_dot(wqvt_ref[v_base:2 * v_base, :], hb)
    cos = cos_ref[...]
    sin = sin_ref[...]
    cos_t = cos_t_ref[...]
    sin_t = sin_t_ref[...]
    for h in range(DIFF_HEADS):
        lo, hi = h * LANES, (h + 1) * LANES
        qt_ref[0, lo:hi, :] = (_rope_rows(qt[lo:hi], cos_t, sin_t) * q_scale).astype(BF16)
    for h in range(DIFF_HEADS):
        lo, hi = h * LANES, (h + 1) * LANES
        k_ref[0, :, lo:hi] = _rope_chunk(k[:, lo:hi], cos, sin).astype(BF16)
    for h in range(DIFF_HEADS):
        lo, hi = h * LANES, (h + 1) * LANES
        _store_vt(vt_ref, h, vt[lo:hi])


def _diff_proj(h, cos, sin, cos_t, sin_t, wk, wqvt, q_scale):
    B, S, D = h.shape
    ts = TOKEN_TILE
    H = DIFF_HEADS
    W = H * LANES
    out_shape = (
        jax.ShapeDtypeStruct((B, W, S), BF16),
        jax.ShapeDtypeStruct((B, S, W), BF16),
        jax.ShapeDtypeStruct((B, H, S // ts, VT_ROWS, ts), BF16),
    )
    return pl.pallas_call(
        functools.partial(_diff_proj_kernel, q_scale=q_scale),
        grid=(B, S // ts),
        in_specs=[
            pl.BlockSpec((1, ts, D), lambda b, s: (b, s, 0)),
            pl.BlockSpec((ts, LANES), lambda b, s: (s, 0)),
            pl.BlockSpec((ts, LANES), lambda b, s: (s, 0)),
            pl.BlockSpec((LANES, ts), lambda b, s: (0, s)),
            pl.BlockSpec((LANES, ts), lambda b, s: (0, s)),
            _const_spec(wk.shape), _const_spec(wqvt.shape),
        ],
        out_specs=(
            pl.BlockSpec((1, W, ts), lambda b, s: (b, 0, s)),
            pl.BlockSpec((1, ts, W), lambda b, s: (b, s, 0)),
            pl.BlockSpec((1, H, 1, VT_ROWS, ts), lambda b, s: (b, 0, s, 0, 0)),
        ),
        out_shape=out_shape,
        compiler_params=pltpu.CompilerParams(
            dimension_semantics=("parallel", "parallel"), vmem_limit_bytes=VMEM_LIMIT),
        name="diff_proj",
    )(h, cos, sin, cos_t, sin_t, wk, wqvt)


def _diff_attn_kernel(q_ref, qn_ref, k_ref, vt_ref, lq1_ref, lk1_ref, lq2_ref, lk2_ref, subln_ref, o_ref,
                      qmap_ref, s_ref, mx_ref, m_ref, acc_ref, *, lambda_init):
    tq = q_ref.shape[2]
    hps = vt_ref.shape[1]
    qi = pl.program_id(2)
    feature = lax.broadcasted_iota(jnp.int32, (LANES, tq), 0)

    def one_map(ref, h, c):
        q = ref[0, h * LANES:(h + 1) * LANES, :]
        keep = (feature < DIFF_HEAD_DIM) if c == 0 else (feature >= DIFF_HEAD_DIM)
        return jnp.where(keep, q, jnp.zeros_like(q))

    for i in range(2 * hps):
        qmap_ref[i] = one_map(q_ref, i // 2, i % 2)
    streams = [(lambda i=i: qmap_ref[i],
                lambda i=i: one_map(qn_ref, i // 2, i % 2),
                lambda j, h=i // 2: k_ref[0, pl.ds(pl.multiple_of(j * tq, tq), tq), h * LANES:(h + 1) * LANES],
                lambda j, h=i // 2: vt_ref[0, h, j]) for i in range(2 * hps)]
    _flash_loop(streams, qi, s_ref, mx_ref, m_ref, acc_ref)
    lam = (jnp.exp(jnp.sum(lq1_ref[...] * lk1_ref[...], axis=-1, keepdims=True))
           - jnp.exp(jnp.sum(lq2_ref[...] * lk2_ref[...], axis=-1, keepdims=True)) + lambda_init)
    gain = subln_ref[...] * (1.0 - lambda_init)
    for h in range(hps):
        o = (_normalised(acc_ref, 2 * h) - lam * _normalised(acc_ref, 2 * h + 1)).T
        o = _rms(o, gain)
        o_ref[0, :, h * LANES:(h + 1) * LANES] = o.astype(BF16)


def _diff_attn(qt, k, vt, lq1, lk1, lq2, lk2, subln, lambda_init):
    B, W, S = qt.shape
    H = DIFF_HEADS
    tq = ATTN_TILE
    hps = DIFF_HEADS_PER_STEP
    nkv = vt.shape[2]
    nq = S // tq
    small = lambda a: _const_spec(a.shape)
    return pl.pallas_call(
        functools.partial(_diff_attn_kernel, lambda_init=lambda_init),
        grid=(B, H // hps, nq),
        in_specs=[
            pl.BlockSpec((1, hps * LANES, tq), lambda b, h, i: (b, h, i)),
            pl.BlockSpec((1, hps * LANES, tq), lambda b, h, i: (b, h, jnp.minimum(i + 1, nq - 1))),
            pl.BlockSpec((1, S, hps * LANES), lambda b, h, i: (b, 0, h)),
            pl.BlockSpec((1, hps, nkv, VT_ROWS, tq), lambda b, h, i: (b, h, 0, 0, 0)),
            small(lq1), small(lk1), small(lq2), small(lk2), small(subln),
        ],
        out_specs=pl.BlockSpec((1, tq, hps * LANES), lambda b, h, i: (b, i, h)),
        out_shape=jax.ShapeDtypeStruct((B, S, H * LANES), BF16),
        scratch_shapes=[
            pltpu.VMEM((2 * hps, LANES, tq), BF16),
            pltpu.VMEM((2 * hps, tq, tq), F32),
            pltpu.VMEM((2 * hps, 1, tq), F32),
            pltpu.VMEM((2 * hps, 1, tq), F32),
            pltpu.VMEM((2 * hps, VT_ROWS, tq), F32),
        ],
        compiler_params=pltpu.CompilerParams(
            dimension_semantics=("parallel", "parallel", "arbitrary"), vmem_limit_bytes=VMEM_LIMIT),
        name="diff_attn",
    )(qt, qt, k, vt, lq1, lk1, lq2, lk2, subln)


def _zero_after(v):
    bits = pltpu.bitcast(v[0:8, 0:LANES], jnp.int32)
    zero = lax.shift_right_logical(lax.shift_right_logical(bits, 16), 16)
    return zero[0:1, 0:1].astype(F32)


def _stage_cast(src_hbm, layer, dst_ref, stage_ref, sem_ref):
    rows = stage_ref.shape[1]
    n_chunks = src_hbm.shape[1] // rows

    def copy(c):
        return pltpu.make_async_copy(src_hbm.at[layer, pl.ds(c * rows, rows), :], stage_ref.at[c % 2],
                                     sem_ref.at[c % 2])

    copy(0).start()
    for c in range(n_chunks):
        if c + 1 < n_chunks:
            copy(c + 1).start()
        copy(c).wait()
        dst_ref[c * rows:(c + 1) * rows, :] = stage_ref[c % 2].astype(BF16)


def _post_kernel(x0_ref, o0_ref, xn_ref, on_ref, wo_ref, g1_ref, b1_ref, wgu_hbm, wd_hbm, g2_ref, b2_ref, out_ref,
                 h1_ref, resid_prev_ref, wgu_ref, wd_ref, stage_gu_ref, stage_d_ref, sem_gu, sem_d, *, d_ff, layer):
    ts = h1_ref.shape[0]

    @pl.when(pl.program_id(0) == 0)
    def _seed():
        _stage_cast(wgu_hbm, layer, wgu_ref, stage_gu_ref, sem_gu)
        _stage_cast(wd_hbm, layer, wd_ref, stage_d_ref, sem_d)
        a0 = _dot(o0_ref[...], wo_ref[...])
        h1_ref[...] = _layernorm(DEEPNORM_ALPHA * x0_ref[...] + a0, g1_ref[...], b1_ref[...])
        resid_prev_ref[...] = jnp.zeros(resid_prev_ref.shape, F32)

    h1 = h1_ref[...]
    hb = h1.astype(BF16)
    attn_next = _dot(on_ref[...], wo_ref[...])

    def ln2_piece(r, zero):
        out_ref[r, :] = _layernorm(resid_prev_ref[r, :] + zero, g2_ref[...], b2_ref[...])

    def ln1_piece(r, zero):
        h1_ref[r, :] = _layernorm(DEEPNORM_ALPHA * xn_ref[r, :] + attn_next[r, :] + zero, g1_ref[...], b1_ref[...])

    pieces = [(fn, slice(r0, r0 + LN_PIECE_ROWS))
              for r0 in range(0, ts, LN_PIECE_ROWS) for fn in (ln2_piece, ln1_piece)]
    chunk_starts = list(range(0, d_ff, FF_CHUNK))
    anchor_chunks = len(chunk_starts) - 1
    acc = jnp.zeros(h1.shape, F32)
    done = 0
    for c, lo in enumerate(chunk_starts):
        hi = min(lo + FF_CHUNK, d_ff)
        g = _dot(hb, wgu_ref[:, lo:hi])
        u = _dot(hb, wgu_ref[:, d_ff + lo:d_ff + hi])
        act = (g * jax.nn.sigmoid(g) * u).astype(BF16)
        acc = acc + _dot(act, wd_ref[lo:hi, :])
        if c < anchor_chunks:
            target = (len(pieces) * (c + 1)) // anchor_chunks
            zero = _zero_after(acc)
            for fn, r in pieces[done:target]:
                fn(r, zero)
            done = target
    resid_prev_ref[...] = DEEPNORM_ALPHA * h1 + acc


def _post(x2d, o2d, wo, g1, b1, wgu_f32, wd_f32, g2, b2, layer):
    T, D = x2d.shape
    ts = TOKEN_TILE
    nt = T // ts
    d_ff = wd_f32.shape[1]
    do = o2d.shape[1]
    assert D % GATE_UP_STAGE_ROWS == 0 and d_ff % DOWN_STAGE_ROWS == 0
    first = lambda t: (0, 0)
    following = lambda t: (jnp.minimum(t + 1, nt - 1), 0)
    return pl.pallas_call(
        functools.partial(_post_kernel, d_ff=d_ff, layer=layer),
        grid=(nt + 1,),
        in_specs=[
            pl.BlockSpec((ts, D), first, pipeline_mode=pl.Buffered(1)),
            pl.BlockSpec((ts, do), first, pipeline_mode=pl.Buffered(1)),
            pl.BlockSpec((ts, D), following),
            pl.BlockSpec((ts, do), following),
            _const_spec(wo.shape, True), _const_spec(g1.shape), _const_spec(b1.shape),
            pl.BlockSpec(memory_space=pl.ANY), pl.BlockSpec(memory_space=pl.ANY),
            _const_spec(g2.shape), _const_spec(b2.shape),
        ],
        out_specs=pl.BlockSpec((ts, D), lambda t: (jnp.maximum(t - 1, 0), 0)),
        out_shape=jax.ShapeDtypeStruct((T, D), F32),
        scratch_shapes=[
            pltpu.VMEM((ts, D), F32), pltpu.VMEM((ts, D), F32),
            pltpu.VMEM((D, 2 * d_ff), BF16), pltpu.VMEM((d_ff, D), BF16),
            pltpu.VMEM((2, GATE_UP_STAGE_ROWS, 2 * d_ff), F32), pltpu.VMEM((2, DOWN_STAGE_ROWS, D), F32),
            pltpu.SemaphoreType.DMA((2,)), pltpu.SemaphoreType.DMA((2,)),
        ],
        compiler_params=pltpu.CompilerParams(
            dimension_semantics=("arbitrary",), vmem_limit_bytes=VMEM_LIMIT),
        name="post",
    )(x2d, o2d, x2d, o2d, wo, g1, b1, wgu_f32, wd_f32, g2, b2)


def _rope_tables(seq_len, dim):
    inv = 1.0 / (ROPE_THETA ** (jnp.arange(0, dim, 2, dtype=F32) / dim))
    ang = jnp.arange(seq_len, dtype=F32)[:, None] * inv[None, :]
    cos = jnp.cos(ang)
    sin = jnp.sin(ang)
    return jnp.concatenate([cos, cos], axis=-1), jnp.concatenate([-sin, sin], axis=-1)


def kernel(x, mla_w_dq, mla_q_norm, mla_w_uq, mla_w_dkv, mla_kv_norm, mla_w_ukv, mla_w_o, kv_w, diff_w_q, diff_lq1, diff_lk1, diff_lq2, diff_lk2, diff_subln, diff_w_o, ln1_g, ln1_b, ln2_g, ln2_b, ffn_w_gate_up, ffn_w_down):
    B, S, D = x.shape
    H = MLA_HEADS
    assert S % ATTN_TILE == 0 and S % TOKEN_TILE == 0 and TOKEN_TILE == ATTN_TILE
    assert MLA_ROPE == DIFF_HEAD_DIM == 64 and MLA_NOPE == MLA_V == LANES and 2 * DIFF_HEAD_DIM == LANES
    assert MLA_HEADS % MLA_HEADS_PER_STEP == 0 and DIFF_HEADS % DIFF_HEADS_PER_STEP == 0

    cos64, sin64 = _rope_tables(S, MLA_ROPE)
    zeros64 = jnp.zeros_like(cos64)
    cos_mla = jnp.concatenate([cos64, zeros64], axis=-1)
    sin_mla = jnp.concatenate([sin64, zeros64], axis=-1)
    cos_diff = jnp.concatenate([cos64, cos64], axis=-1)
    sin_diff = jnp.concatenate([sin64, sin64], axis=-1)
    cos_mla_t, sin_mla_t, cos_diff_t, sin_diff_t = cos_mla.T, sin_mla.T, cos_diff.T, sin_diff.T

    row = lambda v: v.reshape(1, -1).astype(F32)

    w_dkv = mla_w_dkv[0]
    w_in = jnp.concatenate(
        [mla_w_dq[0], w_dkv[:, :MLA_KV_LORA], w_dkv[:, MLA_KV_LORA:],
         jnp.zeros((D, LANES - MLA_ROPE), F32)], axis=1).astype(BF16)
    w_uq = mla_w_uq[0].reshape(MLA_Q_LORA, H, MLA_NOPE + MLA_ROPE)
    w_uq_rope = jnp.pad(w_uq[:, :, MLA_NOPE:], ((0, 0), (0, 0), (0, LANES - MLA_ROPE)))
    w_uqt = jnp.concatenate(
        [w_uq[:, :, :MLA_NOPE].reshape(MLA_Q_LORA, H * MLA_NOPE),
         w_uq_rope.reshape(MLA_Q_LORA, H * LANES)], axis=1).T.astype(BF16)
    w_ukv = mla_w_ukv[0].reshape(MLA_KV_LORA, H, MLA_NOPE + MLA_V)
    w_uk = w_ukv[:, :, :MLA_NOPE].reshape(MLA_KV_LORA, H * MLA_NOPE).astype(BF16)
    w_uvt = w_ukv[:, :, MLA_NOPE:].reshape(MLA_KV_LORA, H * MLA_V).T.astype(BF16)

    mla_scale = (MLA_NOPE + MLA_ROPE) ** -0.5 * LOG2E
    qt, kn, kr, vt = _mla_proj(x, cos_mla, sin_mla, cos_mla_t, sin_mla_t, w_in, row(mla_q_norm[0]),
                          row(mla_kv_norm[0]), w_uqt, w_uk, w_uvt, mla_scale)
    o = _mla_attn(qt, kn, kr, vt)

    x2d = x.reshape(B * S, D)
    h = _post(x2d, o.reshape(B * S, -1), mla_w_o[0].astype(BF16), row(ln1_g[0]), row(ln1_b[0]),
              ffn_w_gate_up, ffn_w_down, row(ln2_g[0]), row(ln2_b[0]), 0)

    qk_width = DIFF_HEADS * 2 * DIFF_HEAD_DIM
    diff_scale = DIFF_HEAD_DIM ** -0.5 * LOG2E
    lambda_init = 0.8 - 0.6 * math.exp(-0.3 * 1)
    w_qvt = jnp.concatenate([diff_w_q[0], kv_w[:, qk_width:]], axis=1).T.astype(BF16)
    dqt, dk, dvt = _diff_proj(h.reshape(B, S, D), cos_diff, sin_diff, cos_diff_t, sin_diff_t,
                              kv_w[:, :qk_width].astype(BF16), w_qvt, diff_scale)
    do = _diff_attn(dqt, dk, dvt, row(diff_lq1[0]), row(diff_lk1[0]), row(diff_lq2[0]), row(diff_lk2[0]),
                    row(diff_subln[0]), lambda_init)
    out = _post(h, do.reshape(B * S, -1), diff_w_o[0].astype(BF16), row(ln1_g[1]), row(ln1_b[1]),
                ffn_w_gate_up, ffn_w_down, row(ln2_g[1]), row(ln2_b[1]), 1)
    return out.reshape(B, S, D)
```

```python
import functools
import math

import jax
import jax.numpy as jnp
from jax import lax
from jax.experimental import pallas as pl
from jax.experimental.pallas import tpu as pltpu

F32 = jnp.float32
BF16 = jnp.bfloat16

MLA_HEADS = 8
MLA_NOPE = 128
MLA_ROPE = 64
MLA_V = 128
MLA_Q_LORA = 384
MLA_KV_LORA = 256
DIFF_HEADS = 8
DIFF_HEAD_DIM = 64
ROPE_THETA = 10000.0
DEPTH = 2
DEEPNORM_ALPHA = (2.0 * DEPTH) ** 0.25
LN_EPS = 1e-5
RMS_EPS = 1e-6
LOG2E = math.log2(math.e)
NEG_BIG = -1e30

LANES = 128
BF16_SUBLANES = 16
TOKEN_TILE = 512
ATTN_TILE = 512
MLA_HEADS_PER_STEP = 4
DIFF_HEADS_PER_STEP = 4
FF_CHUNK = 256
LN_PIECE_ROWS = 64
GATE_UP_STAGE_ROWS = 64
DOWN_STAGE_ROWS = 256
STAGE_SLOTS = 4
VMEM_LIMIT = 56 * 1024 * 1024
VT_ROWS = LANES + BF16_SUBLANES


def _nt_dot(a, b):
    return lax.dot_general(a, b, (((1,), (1,)), ((), ())), preferred_element_type=F32)


def _dot(a, b):
    return jnp.dot(a, b, preferred_element_type=F32)


def _rope_chunk(x, cos, sin_signed):
    lane = lax.broadcasted_iota(jnp.int32, x.shape, 1)
    first_half = (lane % 64) < 32
    rot = jnp.where(first_half, pltpu.roll(x, LANES - 32, 1), pltpu.roll(x, 32, 1))
    return x * cos + rot * sin_signed


def _rope_rows(x, cos_t, sin_signed_t):
    rot = jnp.concatenate([x[32:64], x[0:32], x[96:128], x[64:96]], axis=0)
    return x * cos_t + rot * sin_signed_t


def _rms(x, g):
    return x * lax.rsqrt(jnp.mean(jnp.square(x), axis=-1, keepdims=True) + RMS_EPS) * g


def _layernorm(x, g, b):
    mu = jnp.mean(x, axis=-1, keepdims=True)
    xc = x - mu
    var = jnp.mean(jnp.square(xc), axis=-1, keepdims=True)
    return xc * lax.rsqrt(var + LN_EPS) * g + b


def _const_spec(shape, single_buffer=False):
    nd = len(shape)
    kw = {"pipeline_mode": pl.Buffered(1)} if single_buffer else {}
    return pl.BlockSpec(shape, lambda *_: (0,) * nd, **kw)


def _store_vt(vt_ref, h, vt):
    ts = vt.shape[1]
    vt_ref[0, h, 0, 0:LANES, :] = vt.astype(BF16)
    row = lax.broadcasted_iota(jnp.int32, (BF16_SUBLANES, ts), 0)
    vt_ref[0, h, 0, LANES:VT_ROWS, :] = jnp.where(row == 0, 1.0, 0.0).astype(BF16)


def _mla_proj_kernel(x_ref, cos_ref, sin_ref, cos_t_ref, sin_t_ref, w_in_ref, qn_ref, kvn_ref, w_uqt_ref,
                     w_uk_ref, w_uvt_ref, qt_ref, kn_ref, kr_ref, vt_ref, *, q_scale):
    xb = x_ref[0].astype(BF16)
    y = _dot(xb, w_in_ref[...])
    cq = _rms(y[:, :MLA_Q_LORA], qn_ref[...])
    c = _rms(y[:, MLA_Q_LORA:MLA_Q_LORA + MLA_KV_LORA], kvn_ref[...])
    kr_ref[0] = _rope_chunk(y[:, MLA_Q_LORA + MLA_KV_LORA:], cos_ref[...], sin_ref[...]).astype(BF16)
    cos_t = cos_t_ref[...]
    sin_t = sin_t_ref[...]
    qt = _nt_dot(w_uqt_ref[...], cq.astype(BF16))
    cb = c.astype(BF16)
    k_nope = _dot(cb, w_uk_ref[...])
    vt = _nt_dot(w_uvt_ref[...], cb)
    rope_base = MLA_HEADS * MLA_NOPE
    for h in range(MLA_HEADS):
        lo, hi = h * LANES, (h + 1) * LANES
        qt_ref[0, h, 0:LANES, :] = (qt[lo:hi] * q_scale).astype(BF16)
        qt_ref[0, h, LANES:2 * LANES, :] = (
            _rope_rows(qt[rope_base + lo:rope_base + hi], cos_t, sin_t) * q_scale).astype(BF16)
        kn_ref[0, h] = k_nope[:, lo:hi].astype(BF16)
        _store_vt(vt_ref, h, vt[lo:hi, :])


def _mla_proj(x, cos, sin, cos_t, sin_t, w_in, qn, kvn, w_uqt, w_uk, w_uvt, q_scale):
    B, S, D = x.shape
    ts = TOKEN_TILE
    H = MLA_HEADS
    out_shape = (
        jax.ShapeDtypeStruct((B, H, 2 * LANES, S), BF16),
        jax.ShapeDtypeStruct((B, H, S, LANES), BF16),
        jax.ShapeDtypeStruct((B, S, LANES), BF16),
        jax.ShapeDtypeStruct((B, H, S // ts, VT_ROWS, ts), BF16),
    )
    return pl.pallas_call(
        functools.partial(_mla_proj_kernel, q_scale=q_scale),
        grid=(B, S // ts),
        in_specs=[
            pl.BlockSpec((1, ts, D), lambda b, s: (b, s, 0)),
            pl.BlockSpec((ts, LANES), lambda b, s: (s, 0)),
            pl.BlockSpec((ts, LANES), lambda b, s: (s, 0)),
            pl.BlockSpec((LANES, ts), lambda b, s: (0, s)),
            pl.BlockSpec((LANES, ts), lambda b, s: (0, s)),
            _const_spec(w_in.shape), _const_spec(qn.shape), _const_spec(kvn.shape),
            _const_spec(w_uqt.shape), _const_spec(w_uk.shape), _const_spec(w_uvt.shape),
        ],
        out_specs=(
            pl.BlockSpec((1, H, 2 * LANES, ts), lambda b, s: (b, 0, 0, s)),
            pl.BlockSpec((1, H, ts, LANES), lambda b, s: (b, 0, s, 0)),
            pl.BlockSpec((1, ts, LANES), lambda b, s: (b, s, 0)),
            pl.BlockSpec((1, H, 1, VT_ROWS, ts), lambda b, s: (b, 0, s, 0, 0)),
        ),
        out_shape=out_shape,
        compiler_params=pltpu.CompilerParams(
            dimension_semantics=("parallel", "parallel"), vmem_limit_bytes=VMEM_LIMIT),
        name="mla_proj",
    )(x, cos, sin, cos_t, sin_t, w_in, qn, kvn, w_uqt, w_uk, w_uvt)


def _flash_loop(streams, qi, s_ref, mx_ref, m_ref, acc_ref):
    @pl.when(qi == 0)
    def _seed():
        for i, (q_cur, _, k_block, _) in enumerate(streams):
            s_ref[i] = _dot(k_block(0), q_cur())

    m_ref[...] = jnp.full(m_ref.shape, NEG_BIG, F32)
    acc_ref[...] = jnp.zeros(acc_ref.shape, F32)
    tq = s_ref.shape[2]
    half = tq // 2

    def produce(i, k_blk, q):
        s_new = _dot(k_blk, q)
        s_ref[i] = s_new
        mx_ref[i] = jnp.max(s_new, axis=0, keepdims=True)

    def full_step(j):
        for i, (q_cur, _, k_block, vt_block) in enumerate(streams):
            m_prev = m_ref[i]
            m_new = jnp.maximum(m_prev, mx_ref[i])
            alpha = jnp.exp2(m_prev - m_new)
            p = jnp.exp2(s_ref[i] - m_new).astype(BF16)
            m_ref[i] = m_new
            produce(i, k_block(j + 1), q_cur())
            acc_ref[i] = acc_ref[i] * alpha + _dot(vt_block(j), p)

    def diagonal_step(j):
        row = lax.broadcasted_iota(jnp.int32, (half, half), 0)
        col = lax.broadcasted_iota(jnp.int32, (half, half), 1)
        visible = row <= col
        for i, (_, q_next, k_block, vt_block) in enumerate(streams):
            s_tl = jnp.where(visible, s_ref[i, 0:half, 0:half], NEG_BIG)
            s_tr = s_ref[i, 0:half, half:tq]
            s_br = jnp.where(visible, s_ref[i, half:tq, half:tq], NEG_BIG)
            m_cur = jnp.concatenate(
                [jnp.max(s_tl, axis=0, keepdims=True),
                 jnp.maximum(jnp.max(s_tr, axis=0, keepdims=True), jnp.max(s_br, axis=0, keepdims=True))], axis=1)
            m_prev = m_ref[i]
            m_new = jnp.maximum(m_prev, m_cur)
            alpha = jnp.exp2(m_prev - m_new)
            m_ref[i] = m_new
            p_tl = jnp.exp2(s_tl - m_new[:, 0:half]).astype(BF16)
            p_tr = jnp.exp2(s_tr - m_new[:, half:tq]).astype(BF16)
            p_br = jnp.exp2(s_br - m_new[:, half:tq]).astype(BF16)
            produce(i, k_block(0), q_next())
            vt = vt_block(j)
            acc_ref[i, :, 0:half] = acc_ref[i, :, 0:half] * alpha[:, 0:half] + _dot(vt[:, 0:half], p_tl)
            acc_ref[i, :, half:tq] = (acc_ref[i, :, half:tq] * alpha[:, half:tq]
                                      + _dot(vt[:, 0:half], p_tr) + _dot(vt[:, half:tq], p_br))

    def body(pair, carry):
        full_step(2 * pair)
        full_step(2 * pair + 1)
        return carry

    lax.fori_loop(0, qi // 2, body, 0)

    @pl.when(qi % 2 == 1)
    def _odd_block():
        full_step(qi - 1)

    diagonal_step(qi)


def _normalised(acc_ref, i):
    return acc_ref[i, 0:LANES, :] * (1.0 / acc_ref[i, LANES:LANES + 1, :])


def _mla_attn_kernel(q_ref, qn_ref, kn_ref, kr_ref, vt_ref, o_ref, s_ref, mx_ref, m_ref, acc_ref):
    hps, tq = q_ref.shape[1], q_ref.shape[3]
    qi = pl.program_id(2)

    def key_block(j, h):
        rows = pl.ds(pl.multiple_of(j * tq, tq), tq)
        return jnp.concatenate([kn_ref[0, h, rows, :], kr_ref[0, rows, :]], axis=1)

    streams = [(lambda h=h: q_ref[0, h],
                lambda h=h: qn_ref[0, h],
                lambda j, h=h: key_block(j, h),
                lambda j, h=h: vt_ref[0, h, j]) for h in range(hps)]
    _flash_loop(streams, qi, s_ref, mx_ref, m_ref, acc_ref)
    for h in range(hps):
        o_ref[0, :, h * LANES:(h + 1) * LANES] = _normalised(acc_ref, h).T.astype(BF16)


def _mla_attn(qt, kn, kr, vt):
    B, H, Dk, S = qt.shape
    tq = ATTN_TILE
    hps = MLA_HEADS_PER_STEP
    nkv = vt.shape[2]
    nq = S // tq
    return pl.pallas_call(
        _mla_attn_kernel,
        grid=(B, H // hps, nq),
        in_specs=[
            pl.BlockSpec((1, hps, Dk, tq), lambda b, h, i: (b, h, 0, i)),
            pl.BlockSpec((1, hps, Dk, tq), lambda b, h, i: (b, h, 0, jnp.minimum(i + 1, nq - 1))),
            pl.BlockSpec((1, hps, S, LANES), lambda b, h, i: (b, h, 0, 0)),
            pl.BlockSpec((1, S, LANES), lambda b, h, i: (b, 0, 0)),
            pl.BlockSpec((1, hps, nkv, VT_ROWS, tq), lambda b, h, i: (b, h, 0, 0, 0)),
        ],
        out_specs=pl.BlockSpec((1, tq, hps * LANES), lambda b, h, i: (b, i, h)),
        out_shape=jax.ShapeDtypeStruct((B, S, H * LANES), BF16),
        scratch_shapes=[pltpu.VMEM((hps, tq, tq), F32), pltpu.VMEM((hps, 1, tq), F32), pltpu.VMEM((hps, 1, tq), F32),
                        pltpu.VMEM((hps, VT_ROWS, tq), F32)],
        compiler_params=pltpu.CompilerParams(
            dimension_semantics=("parallel", "parallel", "arbitrary"), vmem_limit_bytes=VMEM_LIMIT),
        name="mla_attn",
    )(qt, qt, kn, kr, vt)


def _diff_proj_kernel(h_ref, cos_ref, sin_ref, cos_t_ref, sin_t_ref, wk_ref, wqvt_ref, qt_ref, k_ref, vt_ref,
                      *, q_scale):
    hb = h_ref[0].astype(BF16)
    v_base = DIFF_HEADS * LANES
    qt = _nt_dot(wqvt_ref[0:v_base, :], hb)
    k = _dot(hb, wk_ref[...])
    vt = _nt_dot(wqvt_ref[v_base:2 * v_base, :], hb)
    cos = cos_ref[...]
    sin = sin_ref[...]
    cos_t = cos_t_ref[...]
    sin_t = sin_t_ref[...]
    for h in range(DIFF_HEADS):
        lo, hi = h * LANES, (h + 1) * LANES
        qt_ref[0, lo:hi, :] = (_rope_rows(qt[lo:hi], cos_t, sin_t) * q_scale).astype(BF16)
    for h in range(DIFF_HEADS):
        lo, hi = h * LANES, (h + 1) * LANES
        k_ref[0, :, lo:hi] = _rope_chunk(k[:, lo:hi], cos, sin).astype(BF16)
    for h in range(DIFF_HEADS):
        lo, hi = h * LANES, (h + 1) * LANES
        _store_vt(vt_ref, h, vt[lo:hi])


def _diff_proj(h, cos, sin, cos_t, sin_t, wk, wqvt, q_scale):
    B, S, D = h.shape
    ts = TOKEN_TILE
    H = DIFF_HEADS
    W = H * LANES
    out_shape = (
        jax.ShapeDtypeStruct((B, W, S), BF16),
        jax.ShapeDtypeStruct((B, S, W), BF16),
        jax.ShapeDtypeStruct((B, H, S // ts, VT_ROWS, ts), BF16),
    )
    return pl.pallas_call(
        functools.partial(_diff_proj_kernel, q_scale=q_scale),
        grid=(B, S // ts),
        in_specs=[
            pl.BlockSpec((1, ts, D), lambda b, s: (b, s, 0)),
            pl.BlockSpec((ts, LANES), lambda b, s: (s, 0)),
            pl.BlockSpec((ts, LANES), lambda b, s: (s, 0)),
            pl.BlockSpec((LANES, ts), lambda b, s: (0, s)),
            pl.BlockSpec((LANES, ts), lambda b, s: (0, s)),
            _const_spec(wk.shape), _const_spec(wqvt.shape),
        ],
        out_specs=(
            pl.BlockSpec((1, W, ts), lambda b, s: (b, 0, s)),
            pl.BlockSpec((1, ts, W), lambda b, s: (b, s, 0)),
            pl.BlockSpec((1, H, 1, VT_ROWS, ts), lambda b, s: (b, 0, s, 0, 0)),
        ),
        out_shape=out_shape,
        compiler_params=pltpu.CompilerParams(
            dimension_semantics=("parallel", "parallel"), vmem_limit_bytes=VMEM_LIMIT),
        name="diff_proj",
    )(h, cos, sin, cos_t, sin_t, wk, wqvt)


def _diff_attn_kernel(q_ref, qn_ref, k_ref, vt_ref, lq1_ref, lk1_ref, lq2_ref, lk2_ref, subln_ref, o_ref,
                      qmap_ref, s_ref, mx_ref, m_ref, acc_ref, *, lambda_init):
    tq = q_ref.shape[2]
    hps = vt_ref.shape[1]
    qi = pl.program_id(2)
    feature = lax.broadcasted_iota(jnp.int32, (LANES, tq), 0)

    def one_map(ref, h, c):
        q = ref[0, h * LANES:(h + 1) * LANES, :]
        keep = (feature < DIFF_HEAD_DIM) if c == 0 else (feature >= DIFF_HEAD_DIM)
        return jnp.where(keep, q, jnp.zeros_like(q))

    for i in range(2 * hps):
        qmap_ref[i] = one_map(q_ref, i // 2, i % 2)
    streams = [(lambda i=i: qmap_ref[i],
                lambda i=i: one_map(qn_ref, i // 2, i % 2),
                lambda j, h=i // 2: k_ref[0, pl.ds(pl.multiple_of(j * tq, tq), tq), h * LANES:(h + 1) * LANES],
                lambda j, h=i // 2: vt_ref[0, h, j]) for i in range(2 * hps)]
    _flash_loop(streams, qi, s_ref, mx_ref, m_ref, acc_ref)
    lam = (jnp.exp(jnp.sum(lq1_ref[...] * lk1_ref[...], axis=-1, keepdims=True))
           - jnp.exp(jnp.sum(lq2_ref[...] * lk2_ref[...], axis=-1, keepdims=True)) + lambda_init)
    gain = subln_ref[...] * (1.0 - lambda_init)
    for h in range(hps):
        o = (_normalised(acc_ref, 2 * h) - lam * _normalised(acc_ref, 2 * h + 1)).T
        o = _rms(o, gain)
        o_ref[0, :, h * LANES:(h + 1) * LANES] = o.astype(BF16)


def _diff_attn(qt, k, vt, lq1, lk1, lq2, lk2, subln, lambda_init):
    B, W, S = qt.shape
    H = DIFF_HEADS
    tq = ATTN_TILE
    hps = DIFF_HEADS_PER_STEP
    nkv = vt.shape[2]
    nq = S // tq
    small = lambda a: _const_spec(a.shape)
    return pl.pallas_call(
        functools.partial(_diff_attn_kernel, lambda_init=lambda_init),
        grid=(B, H // hps, nq),
        in_specs=[
            pl.BlockSpec((1, hps * LANES, tq), lambda b, h, i: (b, h, i)),
            pl.BlockSpec((1, hps * LANES, tq), lambda b, h, i: (b, h, jnp.minimum(i + 1, nq - 1))),
            pl.BlockSpec((1, S, hps * LANES), lambda b, h, i: (b, 0, h)),
            pl.BlockSpec((1, hps, nkv, VT_ROWS, tq), lambda b, h, i: (b, h, 0, 0, 0)),
            small(lq1), small(lk1), small(lq2), small(lk2), small(subln),
        ],
        out_specs=pl.BlockSpec((1, tq, hps * LANES), lambda b, h, i: (b, i, h)),
        out_shape=jax.ShapeDtypeStruct((B, S, H * LANES), BF16),
        scratch_shapes=[
            pltpu.VMEM((2 * hps, LANES, tq), BF16),
            pltpu.VMEM((2 * hps, tq, tq), F32),
            pltpu.VMEM((2 * hps, 1, tq), F32),
            pltpu.VMEM((2 * hps, 1, tq), F32),
            pltpu.VMEM((2 * hps, VT_ROWS, tq), F32),
        ],
        compiler_params=pltpu.CompilerParams(
            dimension_semantics=("parallel", "parallel", "arbitrary"), vmem_limit_bytes=VMEM_LIMIT),
        name="diff_attn",
    )(qt, qt, k, vt, lq1, lk1, lq2, lk2, subln)


def _zero_after(v):
    bits = pltpu.bitcast(v[0:8, 0:LANES], jnp.int32)
    zero = lax.shift_right_logical(lax.shift_right_logical(bits, 16), 16)
    return zero[0:1, 0:1].astype(F32)


def _stage_cast(src_hbm, layer, dst_ref, stage_ref, sem_ref):
    slots, rows = stage_ref.shape[0], stage_ref.shape[1]
    n_chunks = src_hbm.shape[1] // rows

    def copy(c):
        return pltpu.make_async_copy(src_hbm.at[layer, pl.ds(c * rows, rows), :], stage_ref.at[c % slots],
                                     sem_ref.at[c % slots])

    for c in range(min(slots - 1, n_chunks)):
        copy(c).start()
    for c in range(n_chunks):
        ahead = c + slots - 1
        if ahead < n_chunks:
            copy(ahead).start()
        copy(c).wait()
        dst_ref[c * rows:(c + 1) * rows, :] = stage_ref[c % slots].astype(BF16)


def _post_kernel(x0_ref, o0_ref, xn_ref, on_ref, wo_ref, g1_ref, b1_ref, wgu_hbm, wd_hbm, g2_ref, b2_ref, out_ref,
                 h1_ref, resid_prev_ref, wgu_ref, wd_ref, stage_gu_ref, stage_d_ref, sem_gu, sem_d, *, d_ff, layer):
    ts = h1_ref.shape[0]

    @pl.when(pl.program_id(0) == 0)
    def _seed():
        _stage_cast(wgu_hbm, layer, wgu_ref, stage_gu_ref, sem_gu)
        _stage_cast(wd_hbm, layer, wd_ref, stage_d_ref, sem_d)
        a0 = _dot(o0_ref[...], wo_ref[...])
        h1_ref[...] = _layernorm(DEEPNORM_ALPHA * x0_ref[...] + a0, g1_ref[...], b1_ref[...])
        resid_prev_ref[...] = jnp.zeros(resid_prev_ref.shape, F32)

    h1 = h1_ref[...]
    hb = h1.astype(BF16)
    attn_next = _dot(on_ref[...], wo_ref[...])

    def ln2_piece(r, zero):
        out_ref[r, :] = _layernorm(resid_prev_ref[r, :] + zero, g2_ref[...], b2_ref[...])

    def ln1_piece(r, zero):
        h1_ref[r, :] = _layernorm(DEEPNORM_ALPHA * xn_ref[r, :] + attn_next[r, :] + zero, g1_ref[...], b1_ref[...])

    pieces = [(fn, slice(r0, r0 + LN_PIECE_ROWS))
              for r0 in range(0, ts, LN_PIECE_ROWS) for fn in (ln2_piece, ln1_piece)]
    chunk_starts = list(range(0, d_ff, FF_CHUNK))
    anchor_chunks = len(chunk_starts) - 1
    acc = jnp.zeros(h1.shape, F32)
    done = 0
    for c, lo in enumerate(chunk_starts):
        hi = min(lo + FF_CHUNK, d_ff)
        g = _dot(hb, wgu_ref[:, lo:hi])
        u = _dot(hb, wgu_ref[:, d_ff + lo:d_ff + hi])
        act = (g * jax.nn.sigmoid(g) * u).astype(BF16)
        acc = acc + _dot(act, wd_ref[lo:hi, :])
        if c < anchor_chunks:
            target = (len(pieces) * (c + 1)) // anchor_chunks
            zero = _zero_after(acc)
            for fn, r in pieces[done:target]:
                fn(r, zero)
            done = target
    resid_prev_ref[...] = DEEPNORM_ALPHA * h1 + acc


def _post(x2d, o2d, wo, g1, b1, wgu_f32, wd_f32, g2, b2, layer):
    T, D = x2d.shape
    ts = TOKEN_TILE
    nt = T // ts
    d_ff = wd_f32.shape[1]
    do = o2d.shape[1]
    assert D % GATE_UP_STAGE_ROWS == 0 and d_ff % DOWN_STAGE_ROWS == 0
    first = lambda t: (0, 0)
    following = lambda t: (jnp.minimum(t + 1, nt - 1), 0)
    return pl.pallas_call(
        functools.partial(_post_kernel, d_ff=d_ff, layer=layer),
        grid=(nt + 1,),
        in_specs=[
            pl.BlockSpec((ts, D), first, pipeline_mode=pl.Buffered(1)),
            pl.BlockSpec((ts, do), first, pipeline_mode=pl.Buffered(1)),
            pl.BlockSpec((ts, D), following),
            pl.BlockSpec((ts, do), following),
            _const_spec(wo.shape, True), _const_spec(g1.shape), _const_spec(b1.shape),
            pl.BlockSpec(memory_space=pl.ANY), pl.BlockSpec(memory_space=pl.ANY),
            _const_spec(g2.shape), _const_spec(b2.shape),
        ],
        out_specs=pl.BlockSpec((ts, D), lambda t: (jnp.maximum(t - 1, 0), 0)),
        out_shape=jax.ShapeDtypeStruct((T, D), F32),
        scratch_shapes=[
            pltpu.VMEM((ts, D), F32), pltpu.VMEM((ts, D), F32),
            pltpu.VMEM((D, 2 * d_ff), BF16), pltpu.VMEM((d_ff, D), BF16),
            pltpu.VMEM((STAGE_SLOTS, GATE_UP_STAGE_ROWS, 2 * d_ff), F32),
            pltpu.VMEM((STAGE_SLOTS, DOWN_STAGE_ROWS, D), F32),
            pltpu.SemaphoreType.DMA((STAGE_SLOTS,)), pltpu.SemaphoreType.DMA((STAGE_SLOTS,)),
        ],
        compiler_params=pltpu.CompilerParams(
            dimension_semantics=("arbitrary",), vmem_limit_bytes=VMEM_LIMIT),
        name="post",
    )(x2d, o2d, x2d, o2d, wo, g1, b1, wgu_f32, wd_f32, g2, b2)


def _rope_tables(seq_len, dim):
    inv = 1.0 / (ROPE_THETA ** (jnp.arange(0, dim, 2, dtype=F32) / dim))
    ang = jnp.arange(seq_len, dtype=F32)[:, None] * inv[None, :]
    cos = jnp.cos(ang)
    sin = jnp.sin(ang)
    return jnp.concatenate([cos, cos], axis=-1), jnp.concatenate([-sin, sin], axis=-1)


def kernel(x, mla_w_dq, mla_q_norm, mla_w_uq, mla_w_dkv, mla_kv_norm, mla_w_ukv, mla_w_o, kv_w, diff_w_q, diff_lq1, diff_lk1, diff_lq2, diff_lk2, diff_subln, diff_w_o, ln1_g, ln1_b, ln2_g, ln2_b, ffn_w_gate_up, ffn_w_down):
    B, S, D = x.shape
    H = MLA_HEADS
    assert S % ATTN_TILE == 0 and S % TOKEN_TILE == 0 and TOKEN_TILE == ATTN_TILE
    assert MLA_ROPE == DIFF_HEAD_DIM == 64 and MLA_NOPE == MLA_V == LANES and 2 * DIFF_HEAD_DIM == LANES
    assert MLA_HEADS % MLA_HEADS_PER_STEP == 0 and DIFF_HEADS % DIFF_HEADS_PER_STEP == 0

    cos64, sin64 = _rope_tables(S, MLA_ROPE)
    zeros64 = jnp.zeros_like(cos64)
    cos_mla = jnp.concatenate([cos64, zeros64], axis=-1)
    sin_mla = jnp.concatenate([sin64, zeros64], axis=-1)
    cos_diff = jnp.concatenate([cos64, cos64], axis=-1)
    sin_diff = jnp.concatenate([sin64, sin64], axis=-1)
    cos_mla_t, sin_mla_t, cos_diff_t, sin_diff_t = cos_mla.T, sin_mla.T, cos_diff.T, sin_diff.T

    row = lambda v: v.reshape(1, -1).astype(F32)

    w_dkv = mla_w_dkv[0]
    w_in = jnp.concatenate(
        [mla_w_dq[0], w_dkv[:, :MLA_KV_LORA], w_dkv[:, MLA_KV_LORA:],
         jnp.zeros((D, LANES - MLA_ROPE), F32)], axis=1).astype(BF16)
    w_uq = mla_w_uq[0].reshape(MLA_Q_LORA, H, MLA_NOPE + MLA_ROPE)
    w_uq_rope = jnp.pad(w_uq[:, :, MLA_NOPE:], ((0, 0), (0, 0), (0, LANES - MLA_ROPE)))
    w_uqt = jnp.concatenate(
        [w_uq[:, :, :MLA_NOPE].reshape(MLA_Q_LORA, H * MLA_NOPE),
         w_uq_rope.reshape(MLA_Q_LORA, H * LANES)], axis=1).T.astype(BF16)
    w_ukv = mla_w_ukv[0].reshape(MLA_KV_LORA, H, MLA_NOPE + MLA_V)
    w_uk = w_ukv[:, :, :MLA_NOPE].reshape(MLA_KV_LORA, H * MLA_NOPE).astype(BF16)
    w_uvt = w_ukv[:, :, MLA_NOPE:].reshape(MLA_KV_LORA, H * MLA_V).T.astype(BF16)

    mla_scale = (MLA_NOPE + MLA_ROPE) ** -0.5 * LOG2E
    qt, kn, kr, vt = _mla_proj(x, cos_mla, sin_mla, cos_mla_t, sin_mla_t, w_in, row(mla_q_norm[0]),
                          row(mla_kv_norm[0]), w_uqt, w_uk, w_uvt, mla_scale)
    o = _mla_attn(qt, kn, kr, vt)

    x2d = x.reshape(B * S, D)
    h = _post(x2d, o.reshape(B * S, -1), mla_w_o[0].astype(BF16), row(ln1_g[0]), row(ln1_b[0]),
              ffn_w_gate_up, ffn_w_down, row(ln2_g[0]), row(ln2_b[0]), 0)

    qk_width = DIFF_HEADS * 2 * DIFF_HEAD_DIM
    diff_scale = DIFF_HEAD_DIM ** -0.5 * LOG2E
    lambda_init = 0.8 - 0.6 * math.exp(-0.3 * 1)
    w_qvt = jnp.concatenate([diff_w_q[0], kv_w[:, qk_width:]], axis=1).T.astype(BF16)
    dqt, dk, dvt = _diff_proj(h.reshape(B, S, D), cos_diff, sin_diff, cos_diff_t, sin_diff_t,
                              kv_w[:, :qk_width].astype(BF16), w_qvt, diff_scale)
    do = _diff_attn(dqt, dk, dvt, row(diff_lq1[0]), row(diff_lk1[0]), row(diff_lq2[0]), row(diff_lk2[0]),
                    row(diff_subln[0]), lambda_init)
    out = _post(h, do.reshape(B * S, -1), diff_w_o[0].astype(BF16), row(ln1_g[1]), row(ln1_b[1]),
                ffn_w_gate_up, ffn_w_down, row(ln2_g[1]), row(ln2_b[1]), 1)
    return out.reshape(B, S, D)
```
